```python
import functools
import jax, jax.numpy as jnp
from jax import lax
import numpy as np

D_MODEL = 2048
BATCH = 1
SEQ = 8192
DEPTH = 2
DEC_BATCH = 128
DEC_SEQ = 1
PAST_LEN = 16384
PAGE_SIZE = 128

MIX_W = 1024
POOL_GROUPS = 4
POOL_WINDOWS = (2, 4, 8, 16)
POOL_GW = MIX_W // POOL_GROUPS
POOL_HIST = max(POOL_WINDOWS) - 1
MLA_HEADS = 8
MLA_NOPE = 128
MLA_ROPE = 64
MLA_V = 128
MLA_QK = MLA_NOPE + MLA_ROPE
MLA_Q_LORA = 512
MLA_KV_LORA = 256
MLA_SCALE = MLA_QK ** -0.5
ROPE_THETA = 10000.0
ATTN_Q_BLOCK = 128
RWKV_HEAD = 64
RWKV_HEADS = MIX_W // RWKV_HEAD
RWKV_W_LORA = 64
RWKV_A_LORA = 64
RWKV_G_LORA = 128
RWKV_PROJ = 3 * MIX_W + RWKV_W_LORA + RWKV_A_LORA + RWKV_G_LORA
N_BRANCH = 3
MEM_TOKENS = 256
MEM_HEADS = 4
MEM_HEAD_DIM = 128
MEM_W = MEM_HEADS * MEM_HEAD_DIM
D_FF = 5632
N_EXPERTS = 8
TOP_K = 2
D_FF_EXPERT = 2816
N_DENSE = (DEPTH + 1) // 2
N_MOE = DEPTH // 2
RMS_EPS = 1e-6
GN_EPS = 64e-5
NEG_INF = -1e30
OFF_POOL = 0
OFF_CQ = OFF_POOL + MIX_W
OFF_CKV = OFF_CQ + MLA_Q_LORA
OFF_KR = OFF_CKV + MLA_KV_LORA
OFF_RWKV = OFF_KR + MLA_ROPE
OFF_GATE = OFF_RWKV + RWKV_PROJ
W_IN = OFF_GATE + N_BRANCH * D_MODEL

kernel_name = 'hybrid_pool_mla_rwkv7_step'


def rmsnorm(x, g):
    xf = x.astype(jnp.float32)
    y = xf * lax.rsqrt(jnp.mean(xf * xf, axis=-1, keepdims=True) + RMS_EPS)
    return (y * g.astype(jnp.float32)).astype(x.dtype)


def rope_tables(pos):
    inv = ROPE_THETA ** (-jnp.arange(0, MLA_ROPE, 2, dtype=jnp.float32) / MLA_ROPE)
    ang = pos.astype(jnp.float32)[:, None] * inv[None, :]
    return jnp.cos(ang), jnp.sin(ang)


def apply_rope(x, cos, sin):
    xf = x.astype(jnp.float32)
    x1, x2 = jnp.split(xf, 2, axis=-1)
    return jnp.concatenate([x1 * cos - x2 * sin, x2 * cos + x1 * sin], axis=-1).astype(x.dtype)


def pool_mixer(u, hist, pos, w_grp, scale):
    B, T, _ = u.shape
    ext = jnp.concatenate([hist.astype(u.dtype), u], axis=1)
    ext32 = ext.astype(jnp.float32)
    csum = jnp.concatenate([jnp.zeros_like(ext32[:, :1]), jnp.cumsum(ext32, axis=1)], axis=1)
    csum = csum.reshape(B, POOL_HIST + T + 1, POOL_GROUPS, POOL_GW)
    hi = csum[:, POOL_HIST + 1:]
    parts = []
    for g, w in enumerate(POOL_WINDOWS):
        lo = csum[:, POOL_HIST + 1 - w:POOL_HIST + 1 - w + T, g]
        cnt = jnp.minimum(pos + 1, w).astype(jnp.float32)[None, :, None]
        parts.append((hi[:, :, g] - lo) / cnt)
    pooled = jnp.stack(parts, axis=2)
    diff = (pooled - u.reshape(B, T, POOL_GROUPS, POOL_GW).astype(jnp.float32)).astype(u.dtype)
    out = jnp.einsum('btgc,gce->btge', diff, w_grp).reshape(B, T, MIX_W) * scale
    return out, ext[:, T:]


def mla_project(c_q, c_kv, k_r, pos, m):
    B, T, _ = c_q.shape
    cos, sin = rope_tables(pos)
    q = (rmsnorm(c_q, m['cq_g']) @ m['w_uq']).reshape(B, T, MLA_HEADS, MLA_QK)
    q_nope = rmsnorm(q[..., :MLA_NOPE], m['qn_g'])
    q_rope = apply_rope(rmsnorm(q[..., MLA_NOPE:], m['qr_g']), cos[:, None], sin[:, None])
    latent = rmsnorm(c_kv, m['ckv_g'])
    k_rope = apply_rope(rmsnorm(k_r, m['kr_g']), cos, sin)
    return q_nope, q_rope, latent, k_rope


def mla_attend(q_nope, q_rope, k_nope, k_rope, latent, w_uv, q_pos, k_pos):
    s = (jnp.einsum('qhd,khd->hqk', q_nope, k_nope)
         + jnp.einsum('qhr,kr->hqk', q_rope, k_rope)).astype(jnp.float32) * MLA_SCALE
    s = jnp.where(k_pos[None, None, :] <= q_pos[None, :, None], s, NEG_INF)
    p = jax.nn.softmax(s, axis=-1).astype(latent.dtype)
    o_lat = jnp.einsum('hqk,kc->qhc', p, latent)
    return jnp.einsum('qhc,chv->qhv', o_lat, w_uv)


def mla_prompt(q_nope, q_rope, latent, k_rope, pos, m):
    B, S = q_nope.shape[0], q_nope.shape[1]
    nblk = S // ATTN_Q_BLOCK
    k_nope = rmsnorm(jnp.einsum('bsc,chd->bshd', latent, m['w_uk']), m['kn_g'])

    def blocks(t):
        return t.reshape((B, nblk, ATTN_Q_BLOCK) + t.shape[2:]).swapaxes(0, 1)

    def body(args):
        qn, qr, qp = args
        return jax.vmap(mla_attend, in_axes=(0, 0, 0, 0, 0, None, None, None))(
            qn, qr, k_nope, k_rope, latent, m['w_uv'], qp, pos)

    o = lax.map(body, (blocks(q_nope), blocks(q_rope), pos.reshape(nblk, ATTN_Q_BLOCK)))
    return o.swapaxes(0, 1).reshape(B, S, MIX_W)


def mla_sample(q_nope, q_rope, latent, k_rope, pos, cache_lat, cache_rope, layer, page_table, m):
    B, T = q_nope.shape[0], q_nope.shape[1]
    past_len = page_table.shape[1] * PAGE_SIZE
    k_pos = jnp.concatenate([jnp.arange(past_len, dtype=jnp.int32), pos])

    def body(args):
        qn, qr, lat_new, kr_new, pt = args
        lat = jnp.concatenate([cache_lat[layer, pt].reshape(past_len, MLA_KV_LORA), lat_new], axis=0)
        kr = jnp.concatenate([cache_rope[layer, pt].reshape(past_len, MLA_ROPE), kr_new], axis=0)
        kn = rmsnorm(jnp.einsum('kc,chd->khd', lat, m['w_uk']), m['kn_g'])
        return mla_attend(qn, qr, kn, kr, lat, m['w_uv'], pos, k_pos)

    o = lax.map(body, (q_nope, q_rope, latent, k_rope, page_table))
    return o.reshape(B, T, MIX_W)


def rwkv_mixer(z, z_prev, s0, m):
    B, T, _ = z.shape
    f32 = jnp.float32
    z_shift = jnp.concatenate([z_prev[:, None].astype(z.dtype), z[:, :-1]], axis=1)
    zm = z + (z_shift - z) * m['mu']
    cuts = [MIX_W, 2 * MIX_W, 3 * MIX_W, 3 * MIX_W + RWKV_W_LORA, 3 * MIX_W + RWKV_W_LORA + RWKV_A_LORA]
    r, k, v, wl, al, gl = jnp.split(zm, cuts, axis=-1)
    w = -jax.nn.softplus(-(m['w0'] + jnp.tanh(wl) @ m['w2']).astype(f32)) - 0.5
    decay = jnp.exp(-jnp.exp(w))
    a = jax.nn.sigmoid((m['a0'] + al @ m['a2']).astype(f32))
    g = (jax.nn.sigmoid(gl) @ m['g2']).astype(f32)

    def heads(t):
        return t.reshape(B, T, RWKV_HEADS, RWKV_HEAD)

    kf = k.astype(f32)
    kk = heads(kf * m['k_k'].astype(f32))
    kk = kk * lax.rsqrt(jnp.sum(kk * kk, axis=-1, keepdims=True) + 1e-12)
    k_eff = heads(kf * (1.0 + (a - 1.0) * m['k_a'].astype(f32)))
    r_h, v_h, w_h, a_h = heads(r.astype(f32)), heads(v.astype(f32)), heads(decay), heads(a)

    def step(S, inp):
        r_t, w_t, k_t, v_t, kk_t, a_t = inp
        sa = jnp.einsum('bhij,bhj->bhi', S, -kk_t)
        S = S * w_t[:, :, None, :] + sa[..., None] * (kk_t * a_t)[:, :, None, :] + v_t[..., None] * k_t[:, :, None, :]
        return S, jnp.einsum('bhij,bhj->bhi', S, r_t)

    def tm(t):
        return jnp.moveaxis(t, 1, 0)

    s_new, y = lax.scan(step, s0.astype(f32), (tm(r_h), tm(w_h), tm(k_eff), tm(v_h), tm(kk), tm(a_h)))
    y = jnp.moveaxis(y, 0, 1)
    mu_y = jnp.mean(y, axis=-1, keepdims=True)
    var_y = jnp.mean(jnp.square(y - mu_y), axis=-1, keepdims=True)
    yn = ((y - mu_y) * lax.rsqrt(var_y + GN_EPS)).reshape(B, T, MIX_W) * m['ln_w'].astype(f32) + m['ln_b'].astype(f32)
    bonus = (jnp.sum(r_h * k_eff * m['r_k'].astype(f32), axis=-1, keepdims=True) * v_h).reshape(B, T, MIX_W)
    out = ((yn + bonus) * g).astype(z.dtype)
    return out, s_new.astype(s0.dtype), z[:, -1]


def mixer_sublayer(x, pos, pool_hist, z_prev, rwkv_s0, attend_fn, m):
    B, T, _ = x.shape
    h = rmsnorm(x, m['norm_g'])
    proj = h @ m['w_in']
    o_pool, pool_new = pool_mixer(proj[..., OFF_POOL:OFF_CQ], pool_hist, pos, m['pool_w'], m['pool_scale'])
    q_nope, q_rope, latent, k_rope = mla_project(proj[..., OFF_CQ:OFF_CKV], proj[..., OFF_CKV:OFF_KR],
                                                 proj[..., OFF_KR:OFF_RWKV], pos, m)
    o_mla = attend_fn(q_nope, q_rope, latent, k_rope)
    o_rwkv, rwkv_new, z_last = rwkv_mixer(proj[..., OFF_RWKV:OFF_GATE], z_prev, rwkv_s0, m)
    gates = jax.nn.sigmoid(proj[..., OFF_GATE:].reshape(B, T, N_BRANCH, D_MODEL))
    branches = jnp.stack([o_pool, o_mla, o_rwkv], axis=2)
    merged = jnp.sum(gates * jnp.einsum('btnc,ncd->btnd', branches, m['w_branch']), axis=2)
    return x + merged @ m['w_out'], pool_new, latent, k_rope, rwkv_new, z_last


def memory_kv(mem, m):
    B, M, _ = mem.shape
    hm = rmsnorm(mem, m['mem_norm_g'])
    k = rmsnorm((hm @ m['w_k_mem']).reshape(B, M, MEM_HEADS, MEM_HEAD_DIM), m['mem_kn_g'])
    v = (hm @ m['w_v_mem']).reshape(B, M, MEM_HEADS, MEM_HEAD_DIM)
    return k, v


def memory_sublayer(x, mem_k, mem_v, m):
    B, T, _ = x.shape
    h = rmsnorm(x, m['norm_mem_g'])
    q = rmsnorm((h @ m['w_q_mem']).reshape(B, T, MEM_HEADS, MEM_HEAD_DIM), m['mem_qn_g'])
    s = jnp.einsum('bqhd,bkhd->bhqk', q, mem_k).astype(jnp.float32) * (MEM_HEAD_DIM ** -0.5)
    p = jax.nn.softmax(s, axis=-1).astype(mem_v.dtype)
    o = jnp.einsum('bhqk,bkhd->bqhd', p, mem_v).reshape(B, T, MEM_W)
    return x + o @ m['w_o_mem']


def dense_ffn(h, w_gate, w_up, w_down):
    return (jax.nn.silu(h @ w_gate) * (h @ w_up)) @ w_down


def moe_ffn(h, router, router_b, w_gate, w_up, w_down):
    logits = jnp.einsum('btd,de->bte', h, router).astype(jnp.float32) + router_b.astype(jnp.float32)
    top_v, top_i = lax.top_k(logits, TOP_K)
    top_w = jax.nn.softmax(top_v, axis=-1)
    combine = jnp.sum(jax.nn.one_hot(top_i, N_EXPERTS, dtype=jnp.float32) * top_w[..., None], axis=-2).astype(h.dtype)
    y = jnp.zeros_like(h)
    for e in range(N_EXPERTS):
        y = y + combine[..., e:e + 1] * dense_ffn(h, w_gate[e], w_up[e], w_down[e])
    return y


def setup_inputs(seed: int = 0) -> dict:
    key = jax.random.key(seed)
    ks = iter(jax.random.split(key, 96))
    f32 = jnp.float32

    def nrm(shape, scale=1.0):
        return jax.random.normal(next(ks), shape, f32) * scale

    def gain(shape):
        return 1.0 + 0.02 * jax.random.normal(next(ks), shape, f32)

    n_pages = PAST_LEN // PAGE_SIZE
    n_used = DEC_BATCH * n_pages
    n_pool = n_used + max(1, n_used // 4)
    page_table = jax.random.permutation(next(ks), n_pool)[:n_used].reshape(DEC_BATCH, n_pages).astype(jnp.int32)
    return {
        'x_prompt': nrm((BATCH, SEQ, D_MODEL)),
        'x_sample': nrm((DEC_BATCH, DEC_SEQ, D_MODEL)),
        'mem_prompt': nrm((BATCH, MEM_TOKENS, D_MODEL)),
        'state_pool': nrm((DEPTH, DEC_BATCH, POOL_HIST, MIX_W)),
        'cache_mla_latent': nrm((DEPTH, n_pool, PAGE_SIZE, MLA_KV_LORA)),
        'cache_mla_rope': nrm((DEPTH, n_pool, PAGE_SIZE, MLA_ROPE)),
        'state_rwkv': nrm((DEPTH, DEC_BATCH, RWKV_HEADS, RWKV_HEAD, RWKV_HEAD), 0.5),
        'state_rwkv_shift': nrm((DEPTH, DEC_BATCH, RWKV_PROJ)),
        'cache_mem_k': nrm((DEPTH, DEC_BATCH, MEM_TOKENS, MEM_HEADS, MEM_HEAD_DIM)),
        'cache_mem_v': nrm((DEPTH, DEC_BATCH, MEM_TOKENS, MEM_HEADS, MEM_HEAD_DIM)),
        'page_table': page_table,
        'norm_mix_g': gain((DEPTH, D_MODEL)),
        'w_in': nrm((DEPTH, D_MODEL, W_IN), D_MODEL ** -0.5),
        'pool_w': nrm((DEPTH, POOL_GROUPS, POOL_GW, POOL_GW), POOL_GW ** -0.5),
        'pool_scale': gain((DEPTH, MIX_W)),
        'mla_cq_g': gain((DEPTH, MLA_Q_LORA)),
        'mla_w_uq': nrm((DEPTH, MLA_Q_LORA, MLA_HEADS * MLA_QK), MLA_Q_LORA ** -0.5),
        'mla_ckv_g': gain((DEPTH, MLA_KV_LORA)),
        'mla_kr_g': gain((DEPTH, MLA_ROPE)),
        'mla_w_uk': nrm((DEPTH, MLA_KV_LORA, MLA_HEADS, MLA_NOPE), MLA_KV_LORA ** -0.5),
        'mla_w_uv': nrm((DEPTH, MLA_KV_LORA, MLA_HEADS, MLA_V), MLA_KV_LORA ** -0.5),
        'mla_qn_g': gain((DEPTH, MLA_NOPE)),
        'mla_qr_g': gain((DEPTH, MLA_ROPE)),
        'mla_kn_g': gain((DEPTH, MLA_NOPE)),
        'rwkv_mu': jax.random.uniform(next(ks), (DEPTH, RWKV_PROJ), f32),
        'rwkv_w0': nrm((DEPTH, MIX_W), 0.5),
        'rwkv_w2': nrm((DEPTH, RWKV_W_LORA, MIX_W), 0.1),
        'rwkv_a0': nrm((DEPTH, MIX_W), 0.5),
        'rwkv_a2': nrm((DEPTH, RWKV_A_LORA, MIX_W), 0.1),
        'rwkv_g2': nrm((DEPTH, RWKV_G_LORA, MIX_W), RWKV_G_LORA ** -0.5),
        'rwkv_k_k': 0.85 + nrm((DEPTH, MIX_W), 0.05),
        'rwkv_k_a': 1.0 + nrm((DEPTH, MIX_W), 0.05),
        'rwkv_r_k': nrm((DEPTH, RWKV_HEADS, RWKV_HEAD), 0.1),
        'rwkv_ln_w': gain((DEPTH, MIX_W)),
        'rwkv_ln_b': nrm((DEPTH, MIX_W), 0.02),
        'w_branch': nrm((DEPTH, N_BRANCH, MIX_W, D_MODEL), MIX_W ** -0.5),
        'w_out': nrm((DEPTH, D_MODEL, D_MODEL), D_MODEL ** -0.5),
        'norm_mem_g': gain((DEPTH, D_MODEL)),
        'mem_norm_g': gain((DEPTH, D_MODEL)),
        'w_q_mem': nrm((DEPTH, D_MODEL, MEM_W), D_MODEL ** -0.5),
        'w_k_mem': nrm((DEPTH, D_MODEL, MEM_W), D_MODEL ** -0.5),
        'w_v_mem': nrm((DEPTH, D_MODEL, MEM_W), D_MODEL ** -0.5),
        'mem_qn_g': gain((DEPTH, MEM_HEAD_DIM)),
        'mem_kn_g': gain((DEPTH, MEM_HEAD_DIM)),
        'w_o_mem': nrm((DEPTH, MEM_W, D_MODEL), MEM_W ** -0.5),
        'norm_ffn_g': gain((DEPTH, D_MODEL)),
        'ffn_w_gate': nrm((N_DENSE, D_MODEL, D_FF), D_MODEL ** -0.5),
        'ffn_w_up': nrm((N_DENSE, D_MODEL, D_FF), D_MODEL ** -0.5),
        'ffn_w_down': nrm((N_DENSE, D_FF, D_MODEL), D_FF ** -0.5),
        'moe_router': nrm((N_MOE, D_MODEL, N_EXPERTS), D_MODEL ** -0.5),
        'moe_router_b': nrm((N_MOE, N_EXPERTS), 0.01),
        'moe_w_gate': nrm((N_MOE, N_EXPERTS, D_MODEL, D_FF_EXPERT), D_MODEL ** -0.5),
        'moe_w_up': nrm((N_MOE, N_EXPERTS, D_MODEL, D_FF_EXPERT), D_MODEL ** -0.5),
        'moe_w_down': nrm((N_MOE, N_EXPERTS, D_FF_EXPERT, D_MODEL), D_FF_EXPERT ** -0.5),
    }


def reference(x_prompt, x_sample, mem_prompt, state_pool, cache_mla_latent, cache_mla_rope, state_rwkv,
              state_rwkv_shift, cache_mem_k, cache_mem_v, page_table, norm_mix_g, w_in, pool_w, pool_scale,
              mla_cq_g, mla_w_uq, mla_ckv_g, mla_kr_g, mla_w_uk, mla_w_uv, mla_qn_g, mla_qr_g, mla_kn_g,
              rwkv_mu, rwkv_w0, rwkv_w2, rwkv_a0, rwkv_a2, rwkv_g2, rwkv_k_k, rwkv_k_a, rwkv_r_k, rwkv_ln_w,
              rwkv_ln_b, w_branch, w_out, norm_mem_g, mem_norm_g, w_q_mem, w_k_mem, w_v_mem, mem_qn_g, mem_kn_g,
              w_o_mem, norm_ffn_g, ffn_w_gate, ffn_w_up, ffn_w_down, moe_router, moe_router_b, moe_w_gate,
              moe_w_up, moe_w_down):
    bp, seq = x_prompt.shape[0], x_prompt.shape[1]
    dec_seq = x_sample.shape[1]
    past_len = page_table.shape[1] * PAGE_SIZE
    pos_p = jnp.arange(seq, dtype=jnp.int32)
    pos_s = past_len + jnp.arange(dec_seq, dtype=jnp.int32)
    xp, xs = x_prompt, x_sample
    pool_p, lat_p, rope_p, rwkv_p, shift_p, memk_p, memv_p = [], [], [], [], [], [], []
    pool_s, lat_s, rope_s, rwkv_s, shift_s = [], [], [], [], []
    for l in range(DEPTH):
        m = dict(norm_g=norm_mix_g[l], w_in=w_in[l], pool_w=pool_w[l], pool_scale=pool_scale[l],
                 cq_g=mla_cq_g[l], w_uq=mla_w_uq[l], ckv_g=mla_ckv_g[l], kr_g=mla_kr_g[l],
                 w_uk=mla_w_uk[l], w_uv=mla_w_uv[l], qn_g=mla_qn_g[l], qr_g=mla_qr_g[l], kn_g=mla_kn_g[l],
                 mu=rwkv_mu[l], w0=rwkv_w0[l], w2=rwkv_w2[l], a0=rwkv_a0[l], a2=rwkv_a2[l], g2=rwkv_g2[l],
                 k_k=rwkv_k_k[l], k_a=rwkv_k_a[l], r_k=rwkv_r_k[l], ln_w=rwkv_ln_w[l], ln_b=rwkv_ln_b[l],
                 w_branch=w_branch[l], w_out=w_out[l], norm_mem_g=norm_mem_g[l], mem_norm_g=mem_norm_g[l],
                 w_q_mem=w_q_mem[l], w_k_mem=w_k_mem[l], w_v_mem=w_v_mem[l], mem_qn_g=mem_qn_g[l],
                 mem_kn_g=mem_kn_g[l], w_o_mem=w_o_mem[l])
        if l % 2 == 0:
            d = l // 2
            ffn = functools.partial(dense_ffn, w_gate=ffn_w_gate[d], w_up=ffn_w_up[d], w_down=ffn_w_down[d])
        else:
            e = l // 2
            ffn = functools.partial(moe_ffn, router=moe_router[e], router_b=moe_router_b[e],
                                    w_gate=moe_w_gate[e], w_up=moe_w_up[e], w_down=moe_w_down[e])
        attn_p = functools.partial(mla_prompt, pos=pos_p, m=m)
        xp, pn, lt, kr, sn, zn = mixer_sublayer(
            xp, pos_p, jnp.zeros((bp, POOL_HIST, MIX_W), xp.dtype), jnp.zeros((bp, RWKV_PROJ), xp.dtype),
            jnp.zeros((bp, RWKV_HEADS, RWKV_HEAD, RWKV_HEAD), xp.dtype), attn_p, m)
        pool_p.append(pn); lat_p.append(lt); rope_p.append(kr); rwkv_p.append(sn); shift_p.append(zn)
        mk, mv = memory_kv(mem_prompt, m)
        memk_p.append(mk); memv_p.append(mv)
        xp = memory_sublayer(xp, mk, mv, m)
        xp = xp + ffn(rmsnorm(xp, norm_ffn_g[l]))
        attn_s = functools.partial(mla_sample, pos=pos_s, cache_lat=cache_mla_latent, cache_rope=cache_mla_rope,
                                   layer=l, page_table=page_table, m=m)
        xs, pn, lt, kr, sn, zn = mixer_sublayer(xs, pos_s, state_pool[l], state_rwkv_shift[l], state_rwkv[l], attn_s, m)
        pool_s.append(pn); lat_s.append(lt); rope_s.append(kr); rwkv_s.append(sn); shift_s.append(zn)
        xs = memory_sublayer(xs, cache_mem_k[l], cache_mem_v[l], m)
        xs = xs + ffn(rmsnorm(xs, norm_ffn_g[l]))
    return (xp, xs,
            jnp.stack(pool_p), jnp.stack(lat_p), jnp.stack(rope_p), jnp.stack(rwkv_p), jnp.stack(shift_p),
            jnp.stack(memk_p), jnp.stack(memv_p),
            jnp.stack(pool_s), jnp.stack(lat_s), jnp.stack(rope_s), jnp.stack(rwkv_s), jnp.stack(shift_s))
```

```python
import functools

import jax
import jax.numpy as jnp
from jax import lax
from jax.experimental import pallas as pl
from jax.experimental.pallas import tpu as pltpu

f32 = jnp.float32
bf16 = jnp.bfloat16

D_MODEL = 2048
MIX_W = 1024
POOL_WINDOWS = (2, 4, 8, 16)
POOL_GW = 256
POOL_HIST = 15
MLA_HEADS = 8
MLA_NOPE = 128
MLA_ROPE = 64
MLA_QK = MLA_NOPE + MLA_ROPE
MLA_Q_LORA = 512
MLA_KV_LORA = 256
MLA_SCALE = MLA_QK ** -0.5
ROPE_THETA = 10000.0
PAGE_SIZE = 128
RWKV_HEAD = 64
RWKV_HEADS = 16
RWKV_PROJ = 3328
MEM_HEADS = 4
MEM_HEAD_DIM = 128
MEM_W = 512
N_EXPERTS = 8
RMS_EPS = 1e-6
GN_EPS = 64e-5
NEG_INF = -1e30
OFF_CQ, OFF_CKV, OFF_KR, OFF_RWKV, OFF_GATE = 1024, 1536, 1792, 1856, 5184
HEAD_PAD = 256
SCAN_CHUNK = 64
VMEM_LIMIT = 50 * 1024 * 1024
HI = lax.Precision.HIGHEST


def _cp(*sem, vmem=VMEM_LIMIT):
    return pltpu.CompilerParams(dimension_semantics=sem, vmem_limit_bytes=vmem)


def _tile(n, pref, mult=8):
    best = None
    t = mult
    while t <= min(n, pref):
        if n % t == 0:
            best = t
        t += mult
    return best if best is not None else n


def _nt(a, b, precision=None):
    return lax.dot_general(a, b, (((1,), (1,)), ((), ())), precision=precision,
                           preferred_element_type=f32)


def _mm(a, b, precision=None):
    return jnp.dot(a, b, precision=precision, preferred_element_type=f32)


def _rms(x, g):
    return x * lax.rsqrt(jnp.mean(x * x, axis=-1, keepdims=True) + RMS_EPS) * g


def _segsum(x, ind):
    hi = x.astype(bf16)
    lo = (x - hi.astype(f32)).astype(bf16)
    return _mm(hi, ind) + _mm(lo, ind)


def _fmm_norm_kernel(x_ref, g_ref, w_ref, *rest, has_res):
    if has_res:
        r_ref, o_ref, xn_ref = rest
    else:
        o_ref, xn_ref = rest

    @pl.when(pl.program_id(1) == 0)
    def _():
        xn_ref[...] = _rms(x_ref[...], g_ref[...]).astype(bf16)

    acc = _mm(xn_ref[...], w_ref[...])
    if has_res:
        acc = acc + r_ref[...]
    o_ref[...] = acc.astype(o_ref.dtype)


def fmm_norm(x, g, w, *, tm, tn, out_dtype=f32, res=None):
    M, K = x.shape
    N = w.shape[1]
    in_specs = [pl.BlockSpec((tm, K), lambda i, j: (i, 0)),
                pl.BlockSpec((1, K), lambda i, j: (0, 0)),
                pl.BlockSpec((K, tn), lambda i, j: (0, j))]
    args = [x, g, w]
    if res is not None:
        in_specs.append(pl.BlockSpec((tm, tn), lambda i, j: (i, j)))
        args.append(res)
    return pl.pallas_call(
        functools.partial(_fmm_norm_kernel, has_res=res is not None),
        grid=(M // tm, N // tn),
        in_specs=in_specs,
        out_specs=pl.BlockSpec((tm, tn), lambda i, j: (i, j)),
        out_shape=jax.ShapeDtypeStruct((M, N), out_dtype),
        scratch_shapes=[pltpu.VMEM((tm, K), bf16)],
        compiler_params=_cp("parallel", "arbitrary"),
        name="fmm_norm",
    )(*args)


def _fmm_kernel(x_ref, w_ref, *rest, has_res, nk):
    if has_res:
        r_ref, o_ref, acc_ref = rest
    else:
        o_ref, acc_ref = rest
    k = pl.program_id(2)

    @pl.when(k == 0)
    def _():
        acc_ref[...] = jnp.zeros_like(acc_ref)

    acc_ref[...] += _mm(x_ref[...].astype(bf16), w_ref[...])

    @pl.when(k == nk - 1)
    def _():
        a = acc_ref[...]
        if has_res:
            a = a + r_ref[...]
        o_ref[...] = a.astype(o_ref.dtype)


def fmm(x, w, *, tm, tn, tk, out_dtype=f32, res=None):
    M, K = x.shape
    N = w.shape[1]
    nk = K // tk
    in_specs = [pl.BlockSpec((tm, tk), lambda i, j, k: (i, k)),
                pl.BlockSpec((tk, tn), lambda i, j, k: (k, j))]
    args = [x, w]
    if res is not None:
        in_specs.append(pl.BlockSpec((tm, tn), lambda i, j, k: (i, j)))
        args.append(res)
    return pl.pallas_call(
        functools.partial(_fmm_kernel, has_res=res is not None, nk=nk),
        grid=(M // tm, N // tn, nk),
        in_specs=in_specs,
        out_specs=pl.BlockSpec((tm, tn), lambda i, j, k: (i, j)),
        out_shape=jax.ShapeDtypeStruct((M, N), out_dtype),
        scratch_shapes=[pltpu.VMEM((tm, tn), f32)],
        compiler_params=_cp("parallel", "parallel", "arbitrary"),
        name="fmm",
    )(*args)


def _gmm_kernel(x_ref, w_ref, o_ref):
    o_ref[...] = _mm(x_ref[...].astype(bf16), w_ref[...]).astype(o_ref.dtype)


def gmm(x, w, *, tm, out_dtype=f32):
    M = x.shape[0]
    G, K, N = w.shape
    return pl.pallas_call(
        _gmm_kernel,
        grid=(M // tm, G),
        in_specs=[pl.BlockSpec((tm, K), lambda i, g: (i, g)),
                  pl.BlockSpec((None, K, N), lambda i, g: (g, 0, 0))],
        out_specs=pl.BlockSpec((tm, N), lambda i, g: (i, g)),
        out_shape=jax.ShapeDtypeStruct((M, G * N), out_dtype),
        compiler_params=_cp("parallel", "arbitrary"),
        name="gmm",
    )(x, w)


def _pool_p_kernel(u_ref, halo_ref, w_ref, sc_ref, o_ref, ext_ref, *, tm):
    i = pl.program_id(0)
    ext_ref[0:16, :] = jnp.where(i > 0, halo_ref[...], 0.0)
    ext_ref[16:, :] = u_ref[...]
    pos = i * tm + lax.broadcasted_iota(jnp.int32, (tm, 1), 0)
    for g, w in enumerate(POOL_WINDOWS):
        cols = slice(g * POOL_GW, (g + 1) * POOL_GW)
        s = ext_ref[16:, cols]
        for d in range(1, w):
            s = s + ext_ref[pl.ds(16 - d, tm), cols]
        pooled = s / jnp.minimum(pos + 1, w).astype(f32)
        diff = (pooled - ext_ref[16:, cols]).astype(bf16)
        o_ref[:, cols] = (_mm(diff, w_ref[g]) * sc_ref[:, cols]).astype(o_ref.dtype)


def pool_prompt(proj_a, pool_w, scale, *, T, tm):
    return pl.pallas_call(
        functools.partial(_pool_p_kernel, tm=tm),
        grid=(T // tm,),
        in_specs=[pl.BlockSpec((tm, MIX_W), lambda i: (i, 0)),
                  pl.BlockSpec((16, MIX_W), lambda i: (jnp.maximum(i * (tm // 16) - 1, 0), 0)),
                  pl.BlockSpec((4, POOL_GW, POOL_GW), lambda i: (0, 0, 0)),
                  pl.BlockSpec((1, MIX_W), lambda i: (0, 0))],
        out_specs=pl.BlockSpec((tm, MIX_W), lambda i: (i, 0)),
        out_shape=jax.ShapeDtypeStruct((T, MIX_W), bf16),
        scratch_shapes=[pltpu.VMEM((tm + 16, MIX_W), f32)],
        compiler_params=_cp("parallel"),
        name="pool_prompt",
    )(proj_a, proj_a, pool_w, scale)


def _pool_s_kernel(h_ref, u_ref, w_ref, sc_ref, o_ref, *, counts):
    for g, w in enumerate(POOL_WINDOWS):
        cols = slice(g * POOL_GW, (g + 1) * POOL_GW)
        u = u_ref[:, cols]
        s = u
        for d in range(1, w):
            s = s + h_ref[POOL_HIST - d, :, cols]
        diff = (s / counts[g] - u).astype(bf16)
        o_ref[:, cols] = (_mm(diff, w_ref[g]) * sc_ref[:, cols]).astype(o_ref.dtype)


def pool_sample(hist_t, proj_a, pool_w, scale, *, T, B, past_len):
    counts = tuple(float(min(past_len + 1, w)) for w in POOL_WINDOWS)
    return pl.pallas_call(
        functools.partial(_pool_s_kernel, counts=counts),
        grid=(1,),
        in_specs=[pl.BlockSpec((POOL_HIST, B, MIX_W), lambda i: (0, 0, 0)),
                  pl.BlockSpec((B, MIX_W), lambda i: (T // B, 0)),
                  pl.BlockSpec((4, POOL_GW, POOL_GW), lambda i: (0, 0, 0)),
                  pl.BlockSpec((1, MIX_W), lambda i: (0, 0))],
        out_specs=pl.BlockSpec((B, MIX_W), lambda i: (0, 0)),
        out_shape=jax.ShapeDtypeStruct((B, MIX_W), bf16),
        compiler_params=_cp("arbitrary"),
        name="pool_sample",
    )(hist_t, proj_a, pool_w, scale)


def _rope_half(hi, coef):
    rs = lax.rsqrt(jnp.sum(hi * hi, axis=-1, keepdims=True) * (0.5 / MLA_ROPE) + RMS_EPS)
    t = hi * rs * coef
    r = t + pltpu.roll(t, 64, 1)
    lane = lax.broadcasted_iota(jnp.int32, (1, 128), 1)
    return jnp.where(lane < MLA_ROPE, r, 0.0)


def _qprep_kernel(c_ref, g_ref, w_ref, gq_ref, tab_ref, o_ref):
    cn = _rms(c_ref[...], g_ref[...]).astype(bf16)
    x = _mm(cn, w_ref[...])
    gq = gq_ref[...]
    coef = tab_ref[...] * gq[:, MLA_NOPE:]
    for h in range(MLA_HEADS):
        b = h * HEAD_PAD
        o_ref[:, b:b + MLA_NOPE] = _rms(x[:, b:b + MLA_NOPE], gq[:, :MLA_NOPE]).astype(o_ref.dtype)
        o_ref[:, b + MLA_NOPE:b + HEAD_PAD] = _rope_half(x[:, b + MLA_NOPE:b + HEAD_PAD], coef).astype(o_ref.dtype)


def qprep(proj_a, cq_g, w_big, gq, tab, *, tm):
    M = proj_a.shape[0]
    return pl.pallas_call(
        _qprep_kernel,
        grid=(M // tm,),
        in_specs=[pl.BlockSpec((tm, MLA_Q_LORA), lambda i: (i, OFF_CQ // MLA_Q_LORA)),
                  pl.BlockSpec((1, MLA_Q_LORA), lambda i: (0, 0)),
                  pl.BlockSpec((MLA_Q_LORA, MLA_HEADS * HEAD_PAD), lambda i: (0, 0)),
                  pl.BlockSpec((1, HEAD_PAD), lambda i: (0, 0)),
                  pl.BlockSpec((tm, 128), lambda i: (i, 0))],
        out_specs=pl.BlockSpec((tm, MLA_HEADS * HEAD_PAD), lambda i: (i, 0)),
        out_shape=jax.ShapeDtypeStruct((M, MLA_HEADS * HEAD_PAD), bf16),
        compiler_params=_cp("parallel"),
        name="qprep",
    )(proj_a, cq_g, w_big, gq, tab)


def _kprep_kernel(c_ref, g_ref, gk_ref, tab_ref, w_ref, kng_ref, lat_ref, latb_ref, kr_ref, k_ref):
    ck = c_ref[...]
    lat = _rms(ck[:, :MLA_KV_LORA], g_ref[...])
    lat_ref[...] = lat
    latb = lat.astype(bf16)
    latb_ref[...] = latb
    kr = _rope_half(ck[:, MLA_KV_LORA:], tab_ref[...] * gk_ref[...])
    kr_ref[...] = kr
    krb = kr.astype(bf16)
    kn = _mm(latb, w_ref[...])
    for h in range(MLA_HEADS):
        b = h * HEAD_PAD
        k_ref[:, b:b + MLA_NOPE] = _rms(kn[:, h * MLA_NOPE:(h + 1) * MLA_NOPE], kng_ref[...]).astype(bf16)
        k_ref[:, b + MLA_NOPE:b + HEAD_PAD] = krb


def kprep(proj_a, ckv_g, gk, tab, w_uk, kn_g, *, tm):
    M = proj_a.shape[0]
    wk = MLA_KV_LORA + 128
    row = lambda n: pl.BlockSpec((tm, n), lambda i: (i, 0))
    return pl.pallas_call(
        _kprep_kernel,
        grid=(M // tm,),
        in_specs=[pl.BlockSpec((tm, wk), lambda i: (i, OFF_CKV // wk)),
                  pl.BlockSpec((1, MLA_KV_LORA), lambda i: (0, 0)),
                  pl.BlockSpec((1, 128), lambda i: (0, 0)),
                  row(128),
                  pl.BlockSpec((MLA_KV_LORA, MLA_HEADS * MLA_NOPE), lambda i: (0, 0)),
                  pl.BlockSpec((1, MLA_NOPE), lambda i: (0, 0))],
        out_specs=[row(MLA_KV_LORA), row(MLA_KV_LORA), row(128), row(MLA_HEADS * HEAD_PAD)],
        out_shape=[jax.ShapeDtypeStruct((M, MLA_KV_LORA), f32),
                   jax.ShapeDtypeStruct((M, MLA_KV_LORA), bf16),
                   jax.ShapeDtypeStruct((M, 128), f32),
                   jax.ShapeDtypeStruct((M, MLA_HEADS * HEAD_PAD), bf16)],
        compiler_params=_cp("parallel"),
        name="kprep",
    )(proj_a, ckv_g, gk, tab, w_uk, kn_g)


def _attn_p_kernel(q_ref, k_ref, v_ref, o_ref, m_ref, l_ref, acc_ref, *, tq):
    qi = pl.program_id(0)
    ki = pl.program_id(1)

    @pl.when(ki == 0)
    def _():
        m_ref[...] = jnp.full_like(m_ref, NEG_INF)
        l_ref[...] = jnp.zeros_like(l_ref)
        acc_ref[...] = jnp.zeros_like(acc_ref)

    @pl.when(ki <= qi)
    def _():
        rows = qi * tq + lax.broadcasted_iota(jnp.int32, (tq, 1), 0)
        cols = ki * tq + lax.broadcasted_iota(jnp.int32, (1, tq), 1)
        bias = jnp.where(cols <= rows, 0.0, NEG_INF)
        v = v_ref[...]
        for h in range(MLA_HEADS):
            hs = slice(h * HEAD_PAD, (h + 1) * HEAD_PAD)
            s = _nt(q_ref[:, hs], k_ref[:, hs]) + bias
            m_prev = m_ref[h]
            m_new = jnp.maximum(m_prev, jnp.max(s, axis=1, keepdims=True))
            alpha = jnp.exp(m_prev - m_new)
            p = jnp.exp(s - m_new[:, :1])
            l_ref[h] = alpha * l_ref[h] + jnp.sum(p, axis=1, keepdims=True)
            acc_ref[h] = alpha[:, :1] * acc_ref[h] + _mm(p.astype(bf16), v)
            m_ref[h] = m_new

    @pl.when(ki == qi)
    def _():
        for h in range(MLA_HEADS):
            hs = slice(h * HEAD_PAD, (h + 1) * HEAD_PAD)
            o_ref[:, hs] = (acc_ref[h] / l_ref[h][:, :1]).astype(o_ref.dtype)


def attn_prompt(q_full, k_full, lat_b, *, T, tq):
    n = T // tq
    W = MLA_HEADS * HEAD_PAD
    return pl.pallas_call(
        functools.partial(_attn_p_kernel, tq=tq),
        grid=(n, n),
        in_specs=[pl.BlockSpec((tq, W), lambda qi, ki: (qi, 0)),
                  pl.BlockSpec((tq, W), lambda qi, ki: (jnp.minimum(ki, qi), 0)),
                  pl.BlockSpec((tq, MLA_KV_LORA), lambda qi, ki: (jnp.minimum(ki, qi), 0))],
        out_specs=pl.BlockSpec((tq, W), lambda qi, ki: (qi, 0)),
        out_shape=jax.ShapeDtypeStruct((T, W), bf16),
        scratch_shapes=[pltpu.VMEM((MLA_HEADS, tq, 128), f32),
                        pltpu.VMEM((MLA_HEADS, tq, 128), f32),
                        pltpu.VMEM((MLA_HEADS, tq, MLA_KV_LORA), f32)],
        compiler_params=_cp("parallel", "arbitrary"),
        name="attn_prompt",
    )(q_full, k_full, lat_b)


def _attn_s_kernel(pt_ref, qa_ref, qr_ref, qf_ref, kn_ref, ln_ref, wt_ref, *rest, G, ng):
    lat_refs = rest[:G]
    rope_refs = rest[G:2 * G]
    o_ref, m_ref, l_ref, acc_ref = rest[2 * G:]
    g = pl.program_id(1)

    @pl.when(g == 0)
    def _():
        m_ref[...] = jnp.full_like(m_ref, NEG_INF)
        l_ref[...] = jnp.zeros_like(l_ref)
        acc_ref[...] = jnp.zeros_like(acc_ref)

    qa = qa_ref[...]
    qr = qr_ref[...]
    wt = wt_ref[...]
    for i in range(0, G, 2):
        lat = jnp.concatenate([lat_refs[i][...], lat_refs[i + 1][...]], axis=0).astype(bf16)
        rope = jnp.concatenate([rope_refs[i][...], rope_refs[i + 1][...]], axis=0).astype(bf16)
        kt = _nt(wt, lat)
        ssq = jnp.sum((kt * kt).reshape(MLA_HEADS, MLA_NOPE, 2 * PAGE_SIZE), axis=1)
        s = _nt(qa, lat) * lax.rsqrt(ssq * (1.0 / MLA_NOPE) + RMS_EPS) + _nt(qr, rope)
        m_prev = m_ref[...]
        m_new = jnp.maximum(m_prev, jnp.max(s, axis=1, keepdims=True))
        alpha = jnp.exp(m_prev - m_new)
        p = jnp.exp(s - m_new[:, :1])
        l_ref[...] = alpha * l_ref[...] + jnp.sum(p, axis=1, keepdims=True)
        acc_ref[...] = alpha[:, :1] * acc_ref[...] + _mm(p.astype(bf16), lat)
        m_ref[...] = m_new

    @pl.when(g == ng - 1)
    def _():
        s_new = jnp.sum(qf_ref[...].astype(f32) * kn_ref[...].astype(f32), axis=1, keepdims=True)
        m_prev = m_ref[...]
        m_new = jnp.maximum(m_prev, s_new)
        alpha = jnp.exp(m_prev - m_new)
        p = jnp.exp(s_new - m_new)
        l = alpha * l_ref[...] + p
        acc = alpha[:, :1] * acc_ref[...] + p[:, :1] * ln_ref[...].astype(bf16).astype(f32)
        o_ref[...] = (acc / l[:, :1]).astype(o_ref.dtype)


def attn_sample(page_table, qa, qr, qf, knew, latnew, wuk_t, cache_lat, cache_rope, *, layer, G):
    B, n_pages = page_table.shape
    ng = n_pages // G
    pt = page_table.reshape(-1)
    per_b = lambda *shape: pl.BlockSpec((None,) + shape, lambda b, g, pt: (b,) + (0,) * len(shape))

    def page_spec(i, width):
        return pl.BlockSpec((None, None, PAGE_SIZE, width),
                            lambda b, g, pt: (layer, pt[b * n_pages + g * G + i], 0, 0))

    in_specs = [per_b(MLA_HEADS, MLA_KV_LORA), per_b(MLA_HEADS, MLA_ROPE), per_b(MLA_HEADS, HEAD_PAD),
                per_b(MLA_HEADS, HEAD_PAD), per_b(1, MLA_KV_LORA),
                pl.BlockSpec((MLA_HEADS * MLA_NOPE, MLA_KV_LORA), lambda b, g, pt: (0, 0))]
    in_specs += [page_spec(i, MLA_KV_LORA) for i in range(G)]
    in_specs += [page_spec(i, MLA_ROPE) for i in range(G)]
    return pl.pallas_call(
        functools.partial(_attn_s_kernel, G=G, ng=ng),
        grid_spec=pltpu.PrefetchScalarGridSpec(
            num_scalar_prefetch=1,
            grid=(B, ng),
            in_specs=in_specs,
            out_specs=per_b(MLA_HEADS, MLA_KV_LORA),
            scratch_shapes=[pltpu.VMEM((MLA_HEADS, 128), f32),
                            pltpu.VMEM((MLA_HEADS, 128), f32),
                            pltpu.VMEM((MLA_HEADS, MLA_KV_LORA), f32)]),
        out_shape=jax.ShapeDtypeStruct((B, MLA_HEADS, MLA_KV_LORA), bf16),
        compiler_params=_cp("parallel", "arbitrary"),
        name="attn_sample",
    )(pt, qa, qr, qf, knew, latnew, wuk_t, *([cache_lat] * G), *([cache_rope] * G))


def _rwkv_prep_math(z, zs, mu_ref, wcat_ref, w0_ref, a0_ref, kk_ref, ka_ref, rk_ref, ind_ref, outs):
    r_o, k_o, v_o, kk_o, b_o, lw_o, g_o, bonus_o = outs
    zm = z + (zs - z) * mu_ref[...]
    r = zm[:, :MIX_W]
    k = zm[:, MIX_W:2 * MIX_W]
    v = zm[:, 2 * MIX_W:3 * MIX_W]
    tail = zm[:, 3 * MIX_W:]
    lane = lax.broadcasted_iota(jnp.int32, (1, 256), 1)
    act = jnp.where(lane < 64, jnp.tanh(tail), jnp.where(lane < 128, tail, 1.0 / (1.0 + jnp.exp(-tail))))
    lo = _mm(act.astype(bf16), wcat_ref[...])
    u = -(w0_ref[...] + lo[:, :MIX_W])
    w = -(jnp.maximum(u, 0.0) + jnp.log(1.0 + jnp.exp(-jnp.abs(u)))) - 0.5
    a = 1.0 / (1.0 + jnp.exp(-(a0_ref[...] + lo[:, MIX_W:2 * MIX_W])))
    ind = ind_ref[...]
    kk = k * kk_ref[...]
    kk = kk * lax.rsqrt(_segsum(kk * kk, ind) + 1e-12)
    k_eff = k * (1.0 + (a - 1.0) * ka_ref[...])
    r_o[...] = r
    k_o[...] = k_eff
    v_o[...] = v
    kk_o[...] = kk
    b_o[...] = kk * a
    lw_o[...] = -jnp.exp(w)
    g_o[...] = lo[:, 2 * MIX_W:]
    bonus_o[...] = _segsum(r * k_eff * rk_ref[...], ind) * v


def _rwkv_prep_p_kernel(z_ref, halo_ref, *rest):
    params, outs = rest[:8], rest[8:]
    z = z_ref[...]
    first = pl.program_id(0) == 0
    prev = jnp.where(first, 0.0, halo_ref[7:8, :])
    row = lax.broadcasted_iota(jnp.int32, (z.shape[0], 1), 0)
    zs = jnp.where(row == 0, prev, pltpu.roll(z, 1, 0))
    _rwkv_prep_math(z, zs, *params, outs)


def _rwkv_prep_s_kernel(z_ref, zs_ref, *rest):
    params, outs = rest[:8], rest[8:]
    _rwkv_prep_math(z_ref[...], zs_ref[...], *params, outs)


def rwkv_prep(proj_z, shift, params, *, T, B, tm):
    const = lambda a: pl.BlockSpec(a.shape, lambda i: (0,) * a.ndim)
    pspecs = [const(p) for p in params]
    outs_p = [jax.ShapeDtypeStruct((T, MIX_W), f32)] * 8
    res_p = pl.pallas_call(
        _rwkv_prep_p_kernel,
        grid=(T // tm,),
        in_specs=[pl.BlockSpec((tm, RWKV_PROJ), lambda i: (i, 0)),
                  pl.BlockSpec((8, RWKV_PROJ), lambda i: (jnp.maximum(i * (tm // 8) - 1, 0), 0))] + pspecs,
        out_specs=[pl.BlockSpec((tm, MIX_W), lambda i: (i, 0))] * 8,
        out_shape=outs_p,
        compiler_params=_cp("parallel"),
        name="rwkv_prep_prompt",
    )(proj_z, proj_z, *params)
    res_s = pl.pallas_call(
        _rwkv_prep_s_kernel,
        grid=(1,),
        in_specs=[pl.BlockSpec((B, RWKV_PROJ), lambda i: (T // B, 0)),
                  pl.BlockSpec((B, RWKV_PROJ), lambda i: (0, 0))] + pspecs,
        out_specs=[pl.BlockSpec((B, MIX_W), lambda i: (0, 0))] * 8,
        out_shape=[jax.ShapeDtypeStruct((B, MIX_W), f32)] * 8,
        compiler_params=_cp("arbitrary"),
        name="rwkv_prep_sample",
    )(proj_z, shift, *params)
    return res_p, res_s


def _scan_kernel(r_ref, k_ref, v_ref, kk_ref, b_ref, lw_ref, lwt_ref, bt_ref, kt_ref,
                 y_ref, so_ref, st_ref, *, prec):
    C = SCAN_CHUNK

    @pl.when(pl.program_id(0) == 0)
    def _():
        st_ref[...] = jnp.zeros_like(st_ref)

    row = lax.broadcasted_iota(jnp.int32, (C, C), 0)
    col = lax.broadcasted_iota(jnp.int32, (C, C), 1)
    strict = col < row
    incl = col <= row
    lower = incl.astype(f32)
    upper = (row <= col).astype(f32)
    eye = (row == col).astype(f32)
    mm = functools.partial(_mm, precision=prec)
    nt = functools.partial(_nt, precision=prec)
    for h in range(RWKV_HEADS):
        hs = slice(h * RWKV_HEAD, (h + 1) * RWKV_HEAD)
        lw = lw_ref[:, hs]
        cs = _mm(lower, lw, precision=HI)
        cst = _mm(lwt_ref[hs, :], upper, precision=HI)
        em = jnp.exp(-cs)
        emt = jnp.exp(-cst)
        a = -kk_ref[:, hs] * jnp.exp(cs - lw)
        bh = b_ref[:, hs] * em
        kh = k_ref[:, hs] * em
        rh = r_ref[:, hs] * jnp.exp(cs)
        bht = bt_ref[hs, :] * emt
        kht = kt_ref[hs, :] * emt
        gcol = jnp.exp(cst[:, C - 1:C])
        v = v_ref[:, hs]
        pab = jnp.where(strict, nt(a, bh), 0.0)
        pak = jnp.where(strict, nt(a, kh), 0.0)
        prb = jnp.where(incl, nt(rh, bh), 0.0)
        prk = jnp.where(incl, nt(rh, kh), 0.0)
        tinv = eye + pab
        pw = pab
        for _ in range(5):
            pw = mm(pw, pw)
            tinv = tinv + mm(tinv, pw)
        st = st_ref[h]
        ut = mm(tinv, mm(a, st) + mm(pak, v))
        y_ref[:, hs] = mm(rh, st) + mm(prb, ut) + mm(prk, v)
        st_ref[h] = gcol * (st + mm(bht, ut) + mm(kht, v))
    so_ref[...] = st_ref[...]


def rwkv_scan(r, k, v, kk, b, lw, *, prec):
    T = r.shape[0]
    C = SCAN_CHUNK
    nc = T // C
    tr = lambda x: x.reshape(nc, C, MIX_W).transpose(0, 2, 1)
    row = pl.BlockSpec((C, MIX_W), lambda c: (c, 0))
    colm = pl.BlockSpec((None, MIX_W, C), lambda c: (c, 0, 0))
    return pl.pallas_call(
        functools.partial(_scan_kernel, prec=prec),
        grid=(nc,),
        in_specs=[row] * 6 + [colm] * 3,
        out_specs=[row, pl.BlockSpec((RWKV_HEADS, RWKV_HEAD, RWKV_HEAD), lambda c: (0, 0, 0))],
        out_shape=[jax.ShapeDtypeStruct((T, MIX_W), f32),
                   jax.ShapeDtypeStruct((RWKV_HEADS, RWKV_HEAD, RWKV_HEAD), f32)],
        scratch_shapes=[pltpu.VMEM((RWKV_HEADS, RWKV_HEAD, RWKV_HEAD), f32)],
        compiler_params=_cp("arbitrary"),
        name="rwkv_scan",
    )(r, k, v, kk, b, lw, tr(lw), tr(b), tr(k))


def _rwkv_step_kernel(s_ref, r_ref, k_ref, v_ref, kk_ref, b_ref, lw_ref, so_ref, y_ref):
    N = RWKV_HEAD
    eye = (lax.broadcasted_iota(jnp.int32, (N, N), 0) == lax.broadcasted_iota(jnp.int32, (N, N), 1)).astype(f32)
    s = s_ref[...]
    sa = jnp.sum(s * (-kk_ref[...]), axis=-1, keepdims=True)
    vcol = jnp.sum(eye * v_ref[...], axis=-1, keepdims=True)
    s_new = s * jnp.exp(lw_ref[...]) + sa * b_ref[...] + vcol * k_ref[...]
    so_ref[...] = s_new
    ycol = jnp.sum(s_new * r_ref[...], axis=-1, keepdims=True)
    y_ref[...] = jnp.sum(ycol * eye, axis=-2, keepdims=True)


def rwkv_step(state, r, k, v, kk, b, lw, *, tb):
    B = state.shape[0]
    H, N = RWKV_HEADS, RWKV_HEAD
    vec = lambda x: x.reshape(B, H, 1, N)
    vspec = pl.BlockSpec((tb, H, 1, N), lambda i: (i, 0, 0, 0))
    sspec = pl.BlockSpec((tb, H, N, N), lambda i: (i, 0, 0, 0))
    s_new, y = pl.pallas_call(
        _rwkv_step_kernel,
        grid=(B // tb,),
        in_specs=[sspec] + [vspec] * 6,
        out_specs=[sspec, vspec],
        out_shape=[jax.ShapeDtypeStruct((B, H, N, N), f32), jax.ShapeDtypeStruct((B, H, 1, N), f32)],
        compiler_params=_cp("parallel"),
        name="rwkv_step",
    )(state, vec(r), vec(k), vec(v), vec(kk), vec(b), vec(lw))
    return s_new, y.reshape(B, MIX_W)


def _rwkv_post_kernel(y_ref, bonus_ref, g_ref, lnw_ref, lnb_ref, ind_ref, o_ref):
    ind = ind_ref[...]
    y = y_ref[...]
    d = y - _segsum(y, ind) * (1.0 / RWKV_HEAD)
    var = _segsum(d * d, ind) * (1.0 / RWKV_HEAD)
    yn = d * lax.rsqrt(var + GN_EPS) * lnw_ref[...] + lnb_ref[...]
    o_ref[...] = ((yn + bonus_ref[...]) * g_ref[...]).astype(o_ref.dtype)


def rwkv_post(y, bonus, g, ln_w, ln_b, ind, *, tm):
    M = y.shape[0]
    row = pl.BlockSpec((tm, MIX_W), lambda i: (i, 0))
    vec = pl.BlockSpec((1, MIX_W), lambda i: (0, 0))
    return pl.pallas_call(
        _rwkv_post_kernel,
        grid=(M // tm,),
        in_specs=[row, row, row, vec, vec, pl.BlockSpec((MIX_W, MIX_W), lambda i: (0, 0))],
        out_specs=row,
        out_shape=jax.ShapeDtypeStruct((M, MIX_W), bf16),
        compiler_params=_cp("parallel"),
        name="rwkv_post",
    )(y, bonus, g, ln_w, ln_b, ind)


def _merge_kernel(b0_ref, b1_ref, b2_ref, w_ref, g0_ref, g1_ref, g2_ref, o_ref):
    acc = None
    for n, (b_ref, g_ref) in enumerate(((b0_ref, g0_ref), (b1_ref, g1_ref), (b2_ref, g2_ref))):
        gate = 1.0 / (1.0 + jnp.exp(-g_ref[...]))
        t = gate * _mm(b_ref[...], w_ref[n])
        acc = t if acc is None else acc + t
    o_ref[...] = acc.astype(o_ref.dtype)


def merge(branches, w_branch, proj_g, *, tm, tn):
    M = proj_g.shape[0]
    nb = D_MODEL // tn
    bspec = pl.BlockSpec((tm, MIX_W), lambda i, j: (i, 0))
    gspec = lambda n: pl.BlockSpec((tm, tn), lambda i, j: (i, n * nb + j))
    return pl.pallas_call(
        _merge_kernel,
        grid=(M // tm, nb),
        in_specs=[bspec, bspec, bspec, pl.BlockSpec((3, MIX_W, tn), lambda i, j: (0, 0, j)),
                  gspec(0), gspec(1), gspec(2)],
        out_specs=pl.BlockSpec((tm, tn), lambda i, j: (i, j)),
        out_shape=jax.ShapeDtypeStruct((M, D_MODEL), bf16),
        compiler_params=_cp("parallel", "arbitrary"),
        name="merge",
    )(*branches, w_branch, proj_g, proj_g, proj_g)


def _memkv_kernel(kv_ref, g_ref, k_ref, v_ref):
    for h in range(MEM_HEADS):
        hs = slice(h * MEM_HEAD_DIM, (h + 1) * MEM_HEAD_DIM)
        k_ref[:, hs] = _rms(kv_ref[:, hs], g_ref[...])
    v_ref[...] = kv_ref[:, MEM_W:]


def memkv_post(kv, g):
    n = kv.shape[0]
    return pl.pallas_call(
        _memkv_kernel,
        grid=(1,),
        in_specs=[pl.BlockSpec((n, 2 * MEM_W), lambda i: (0, 0)), pl.BlockSpec((1, MEM_HEAD_DIM), lambda i: (0, 0))],
        out_specs=[pl.BlockSpec((n, MEM_W), lambda i: (0, 0))] * 2,
        out_shape=[jax.ShapeDtypeStruct((n, MEM_W), f32)] * 2,
        compiler_params=_cp("arbitrary"),
        name="memkv_post",
    )(kv, g)


def _memattn_p_kernel(q_ref, g_ref, k_ref, v_ref, o_ref):
    for h in range(MEM_HEADS):
        hs = slice(h * MEM_HEAD_DIM, (h + 1) * MEM_HEAD_DIM)
        q = _rms(q_ref[:, hs], g_ref[...]).astype(bf16)
        s = _nt(q, k_ref[:, hs].astype(bf16)) * (MEM_HEAD_DIM ** -0.5)
        p = jnp.exp(s - jnp.max(s, axis=1, keepdims=True))
        o = _mm(p.astype(bf16), v_ref[:, hs].astype(bf16)) / jnp.sum(p, axis=1, keepdims=True)
        o_ref[:, hs] = o.astype(o_ref.dtype)


def memattn_prompt(q, g, mem_k, mem_v, *, T, tm):
    n = mem_k.shape[0]
    return pl.pallas_call(
        _memattn_p_kernel,
        grid=(T // tm,),
        in_specs=[pl.BlockSpec((tm, MEM_W), lambda i: (i, 0)), pl.BlockSpec((1, MEM_HEAD_DIM), lambda i: (0, 0)),
                  pl.BlockSpec((n, MEM_W), lambda i: (0, 0)), pl.BlockSpec((n, MEM_W), lambda i: (0, 0))],
        out_specs=pl.BlockSpec((tm, MEM_W), lambda i: (i, 0)),
        out_shape=jax.ShapeDtypeStruct((T, MEM_W), bf16),
        compiler_params=_cp("parallel"),
        name="memattn_prompt",
    )(q, g, mem_k, mem_v)


def _memattn_s_kernel(q_ref, g_ref, k_ref, v_ref, o_ref):
    for h in range(MEM_HEADS):
        hs = slice(h * MEM_HEAD_DIM, (h + 1) * MEM_HEAD_DIM)
        q = _rms(q_ref[:, :, hs], g_ref[...])
        s = jnp.sum(k_ref[:, :, hs] * q, axis=-1, keepdims=True) * (MEM_HEAD_DIM ** -0.5)
        p = jnp.exp(s - jnp.max(s, axis=1, keepdims=True))
        o = jnp.sum(p * v_ref[:, :, hs], axis=1, keepdims=True) / jnp.sum(p, axis=1, keepdims=True)
        o_ref[:, :, hs] = o.astype(o_ref.dtype)


def memattn_sample(q, g, cache_k, cache_v, *, layer, T, B, tb):
    n = cache_k.shape[2]
    q3 = q.reshape(q.shape[0], 1, MEM_W)
    ck = cache_k.reshape(cache_k.shape[0], B, n, MEM_W)
    cv = cache_v.reshape(cache_v.shape[0], B, n, MEM_W)
    kvspec = pl.BlockSpec((None, tb, n, MEM_W), lambda i: (layer, i, 0, 0))
    out = pl.pallas_call(
        _memattn_s_kernel,
        grid=(B // tb,),
        in_specs=[pl.BlockSpec((tb, 1, MEM_W), lambda i: (T // tb + i, 0, 0)),
                  pl.BlockSpec((1, MEM_HEAD_DIM), lambda i: (0, 0)), kvspec, kvspec],
        out_specs=pl.BlockSpec((tb, 1, MEM_W), lambda i: (i, 0, 0)),
        out_shape=jax.ShapeDtypeStruct((B, 1, MEM_W), bf16),
        compiler_params=_cp("parallel"),
        name="memattn_sample",
    )(q3, g, ck, cv)
    return out.reshape(B, MEM_W)


def _ffn1_kernel(x_ref, g_ref, wg_ref, wu_ref, *rest, scaled):
    if scaled:
        c_ref, o_ref, xn_ref = rest
    else:
        o_ref, xn_ref = rest

    @pl.when(pl.program_id(1) == 0)
    def _():
        xn_ref[...] = _rms(x_ref[...], g_ref[...]).astype(bf16)

    xn = xn_ref[...]
    a = _mm(xn, wg_ref[...])
    a = a / (1.0 + jnp.exp(-a)) * _mm(xn, wu_ref[...])
    if scaled:
        a = a * c_ref[:, :1]
    o_ref[...] = a.astype(o_ref.dtype)


def ffn1(x, g, wg, wu, *, tm, tn, comb=None, per_expert=None):
    M, K = x.shape
    F = wg.shape[1]
    in_specs = [pl.BlockSpec((tm, K), lambda i, j: (i, 0)), pl.BlockSpec((1, K), lambda i, j: (0, 0)),
                pl.BlockSpec((K, tn), lambda i, j: (0, j)), pl.BlockSpec((K, tn), lambda i, j: (0, j))]
    args = [x, g, wg, wu]
    if comb is not None:
        nper = per_expert // tn
        in_specs.append(pl.BlockSpec((tm, 128), lambda i, j: (i, j // nper)))
        args.append(comb)
    return pl.pallas_call(
        functools.partial(_ffn1_kernel, scaled=comb is not None),
        grid=(M // tm, F // tn),
        in_specs=in_specs,
        out_specs=pl.BlockSpec((tm, tn), lambda i, j: (i, j)),
        out_shape=jax.ShapeDtypeStruct((M, F), bf16),
        scratch_shapes=[pltpu.VMEM((tm, K), bf16)],
        compiler_params=_cp("parallel", "arbitrary"),
        name="ffn1",
    )(*args)


def _router_kernel(x_ref, g_ref, w_ref, b_ref, e_ref, o_ref):
    logits = _mm(_rms(x_ref[...], g_ref[...]), w_ref[...], precision=HI) + b_ref[...]
    lane = lax.broadcasted_iota(jnp.int32, logits.shape, 1)
    logits = jnp.where(lane < N_EXPERTS, logits, -jnp.inf)
    m1 = jnp.max(logits, axis=1, keepdims=True)
    i1 = jnp.min(jnp.where(logits == m1, lane, 128), axis=1, keepdims=True)
    rest = jnp.where(lane == i1, -jnp.inf, logits)
    m2 = jnp.max(rest, axis=1, keepdims=True)
    i2 = jnp.min(jnp.where(rest == m2, lane, 128), axis=1, keepdims=True)
    e2 = jnp.exp(m2 - m1)
    w1 = 1.0 / (1.0 + e2)
    w2 = e2 / (1.0 + e2)
    comb = jnp.where(lane == i1, w1, 0.0) + jnp.where(lane == i2, w2, 0.0)
    o_ref[...] = _mm(comb, e_ref[...], precision=HI)


def router(x, g, w_pad, b_pad, expand, *, tm):
    M, K = x.shape
    return pl.pallas_call(
        _router_kernel,
        grid=(M // tm,),
        in_specs=[pl.BlockSpec((tm, K), lambda i: (i, 0)), pl.BlockSpec((1, K), lambda i: (0, 0)),
                  pl.BlockSpec((K, 128), lambda i: (0, 0)), pl.BlockSpec((1, 128), lambda i: (0, 0)),
                  pl.BlockSpec((128, N_EXPERTS * 128), lambda i: (0, 0))],
        out_specs=pl.BlockSpec((tm, N_EXPERTS * 128), lambda i: (i, 0)),
        out_shape=jax.ShapeDtypeStruct((M, N_EXPERTS * 128), f32),
        compiler_params=_cp("parallel"),
        name="router",
    )(x, g, w_pad, b_pad, expand)


def _swap_halves(x):
    h = x.shape[-1] // 2
    return jnp.concatenate([x[..., h:], x[..., :h]], axis=-1)


def kernel(x_prompt, x_sample, mem_prompt, state_pool, cache_mla_latent, cache_mla_rope, state_rwkv, state_rwkv_shift, cache_mem_k, cache_mem_v, page_table, norm_mix_g, w_in, pool_w, pool_scale, mla_cq_g, mla_w_uq, mla_ckv_g, mla_kr_g, mla_w_uk, mla_w_uv, mla_qn_g, mla_qr_g, mla_kn_g, rwkv_mu, rwkv_w0, rwkv_w2, rwkv_a0, rwkv_a2, rwkv_g2, rwkv_k_k, rwkv_k_a, rwkv_r_k, rwkv_ln_w, rwkv_ln_b, w_branch, w_out, norm_mem_g, mem_norm_g, w_q_mem, w_k_mem, w_v_mem, mem_qn_g, mem_kn_g, w_o_mem, norm_ffn_g, ffn_w_gate, ffn_w_up, ffn_w_down, moe_router, moe_router_b, moe_w_gate, moe_w_up, moe_w_down):
    depth = w_in.shape[0]
    T = x_prompt.shape[1]
    B = x_sample.shape[0]
    assert x_prompt.shape[0] == 1 and x_sample.shape[1] == 1
    n_pages = page_table.shape[1]
    past_len = n_pages * PAGE_SIZE
    M = T + B
    assert T % B == 0 and T % SCAN_CHUNK == 0 and B % 8 == 0

    tm = _tile(M, 640)
    tp = _tile(T, 512)
    tq = _tile(T, 512, 128)
    tb = _tile(B, 8)
    G = _tile(n_pages, 16, 2)
    row = lambda v: v.reshape(1, -1)

    x = jnp.concatenate([x_prompt[0], x_sample[:, 0]], axis=0)

    pos = jnp.concatenate([jnp.arange(T, dtype=jnp.int32), jnp.full((B,), past_len, jnp.int32)])
    inv = ROPE_THETA ** (-jnp.arange(0, MLA_ROPE, 2, dtype=f32) / MLA_ROPE)
    ang = pos.astype(f32)[:, None] * inv[None, :]
    cos, sin = jnp.cos(ang), jnp.sin(ang)
    tab = jnp.concatenate([cos, cos, -sin, sin], axis=1)

    ids = jnp.arange(MIX_W) // RWKV_HEAD
    ind = (ids[:, None] == ids[None, :]).astype(bf16)
    expand = (jnp.arange(128)[:, None] == (jnp.arange(N_EXPERTS * 128) // 128)[None, :]).astype(f32)

    outs = {k: [] for k in ("pool_p", "lat_p", "rope_p", "rwkv_p", "shift_p", "memk_p", "memv_p",
                            "pool_s", "lat_s", "rope_s", "rwkv_s", "shift_s")}
    for l in range(depth):
        wi = w_in[l]
        kr_cols = wi[:, OFF_KR:OFF_RWKV]
        w_a = jnp.concatenate([wi[:, :OFF_KR], kr_cols, _swap_halves(kr_cols)], axis=1).astype(bf16)
        w_z = wi[:, OFF_RWKV:OFF_GATE].astype(bf16)
        w_g = wi[:, OFF_GATE:].astype(bf16)
        uq = mla_w_uq[l].reshape(MLA_Q_LORA, MLA_HEADS, MLA_QK)
        w_big = jnp.concatenate([uq, _swap_halves(uq[..., MLA_NOPE:])], axis=-1)
        w_big = w_big.reshape(MLA_Q_LORA, MLA_HEADS * HEAD_PAD).astype(bf16)
        gq = row(jnp.concatenate([mla_qn_g[l], mla_qr_g[l], _swap_halves(mla_qr_g[l])]) * MLA_SCALE)
        gk = row(jnp.concatenate([mla_kr_g[l], _swap_halves(mla_kr_g[l])]))
        w_uk = mla_w_uk[l].reshape(MLA_KV_LORA, MLA_HEADS * MLA_NOPE)
        w_uk_b = w_uk.astype(bf16)
        wuk_t = w_uk.T.astype(bf16)
        w_abs = (mla_w_uk[l] * mla_kn_g[l][None, None, :]).transpose(1, 2, 0)
        w_abs = jnp.concatenate([w_abs, jnp.zeros((MLA_HEADS, HEAD_PAD - MLA_NOPE, MLA_KV_LORA), f32)], axis=1).astype(bf16)
        w_uv = mla_w_uv[l].transpose(1, 0, 2).astype(bf16)
        wcat = jnp.zeros((256, 3 * MIX_W), f32)
        wcat = wcat.at[:64, :MIX_W].set(rwkv_w2[l]).at[64:128, MIX_W:2 * MIX_W].set(rwkv_a2[l])
        wcat = wcat.at[128:, 2 * MIX_W:].set(rwkv_g2[l]).astype(bf16)
        rparams = [row(rwkv_mu[l]), wcat, row(rwkv_w0[l]), row(rwkv_a0[l]), row(rwkv_k_k[l]), row(rwkv_k_a[l]),
                   row(rwkv_r_k[l]), ind]

        g_mix = row(norm_mix_g[l])
        proj_a = fmm_norm(x, g_mix, w_a, tm=tm, tn=w_a.shape[1])
        proj_z = fmm_norm(x, g_mix, w_z, tm=tm, tn=RWKV_PROJ // 2)
        proj_g = fmm_norm(x, g_mix, w_g, tm=tm, tn=1536)

        pw = pool_w[l].astype(bf16)
        o_pool = jnp.concatenate([
            pool_prompt(proj_a, pw, row(pool_scale[l]), T=T, tm=tp),
            pool_sample(state_pool[l].transpose(1, 0, 2), proj_a, pw, row(pool_scale[l]), T=T, B=B, past_len=past_len)])
        u = proj_a[:, :MIX_W]
        outs["pool_p"].append(u[T - POOL_HIST:T][None])
        outs["pool_s"].append(jnp.concatenate([state_pool[l][:, 1:], u[T:, None]], axis=1))

        q_full = qprep(proj_a, row(mla_cq_g[l]), w_big, gq, tab, tm=tm)
        lat, lat_b, krope, k_full = kprep(proj_a, row(mla_ckv_g[l]), gk, tab, w_uk_b, row(mla_kn_g[l]), tm=tm)
        outs["lat_p"].append(lat[:T][None])
        outs["lat_s"].append(lat[T:, None])
        outs["rope_p"].append(krope[:T, :MLA_ROPE][None])
        outs["rope_s"].append(krope[T:, None, :MLA_ROPE])
        o_lat_p = attn_prompt(q_full, k_full, lat_b, T=T, tq=tq)
        q_s = q_full[T:]
        qa = gmm(q_s, w_abs, tm=B, out_dtype=bf16).reshape(B, MLA_HEADS, MLA_KV_LORA)
        q_s3 = q_s.reshape(B, MLA_HEADS, HEAD_PAD)
        o_lat_s = attn_sample(page_table, qa, q_s3[:, :, MLA_NOPE:MLA_QK], q_s3,
                              k_full[T:].reshape(B, MLA_HEADS, HEAD_PAD), lat[T:, None], wuk_t,
                              cache_mla_latent, cache_mla_rope, layer=l, G=G)
        o_lat = jnp.concatenate([o_lat_p, o_lat_s.reshape(B, MLA_HEADS * MLA_KV_LORA)])
        o_mla = gmm(o_lat, w_uv, tm=tm, out_dtype=bf16)

        (rp, rs_) = rwkv_prep(proj_z, state_rwkv_shift[l], rparams, T=T, B=B, tm=_tile(T, 256))
        y_p, st_t = rwkv_scan(rp[0], rp[1], rp[2], rp[3], rp[4], rp[5], prec=HI)
        s_new, y_s = rwkv_step(state_rwkv[l], rs_[0], rs_[1], rs_[2], rs_[3], rs_[4], rs_[5], tb=tb)
        outs["rwkv_p"].append(st_t.transpose(0, 2, 1)[None])
        outs["rwkv_s"].append(s_new)
        outs["shift_p"].append(proj_z[T - 1:T])
        outs["shift_s"].append(proj_z[T:])
        cat = lambda a, b: jnp.concatenate([a, b])
        o_rwkv = rwkv_post(cat(y_p, y_s), cat(rp[7], rs_[7]), cat(rp[6], rs_[6]),
                           row(rwkv_ln_w[l]), row(rwkv_ln_b[l]), ind, tm=tm)

        merged = merge((o_pool, o_mla, o_rwkv), w_branch[l].astype(bf16), proj_g, tm=tm, tn=512)
        x = fmm(merged, w_out[l].astype(bf16), tm=tm, tn=1024, tk=D_MODEL, res=x)

        w_kv = jnp.concatenate([w_k_mem[l], w_v_mem[l]], axis=1).astype(bf16)
        n_mem = mem_prompt.shape[1]
        kv = fmm_norm(mem_prompt[0], row(mem_norm_g[l]), w_kv, tm=n_mem, tn=2 * MEM_W)
        mem_k, mem_v = memkv_post(kv, row(mem_kn_g[l]))
        outs["memk_p"].append(mem_k.reshape(1, n_mem, MEM_HEADS, MEM_HEAD_DIM))
        outs["memv_p"].append(mem_v.reshape(1, n_mem, MEM_HEADS, MEM_HEAD_DIM))
        q_mem = fmm_norm(x, row(norm_mem_g[l]), w_q_mem[l].astype(bf16), tm=tm, tn=MEM_W)
        o_mem = jnp.concatenate([
            memattn_prompt(q_mem, row(mem_qn_g[l]), mem_k, mem_v, T=T, tm=tp),
            memattn_sample(q_mem, row(mem_qn_g[l]), cache_mem_k, cache_mem_v, layer=l, T=T, B=B, tb=tb)])
        x = fmm(o_mem, w_o_mem[l].astype(bf16), tm=tm, tn=1024, tk=MEM_W, res=x)

        g_ffn = row(norm_ffn_g[l])
        if l % 2 == 0:
            d = l // 2
            a = ffn1(x, g_ffn, ffn_w_gate[d].astype(bf16), ffn_w_up[d].astype(bf16), tm=tm, tn=1408)
            x = fmm(a, ffn_w_down[d].astype(bf16), tm=tm, tn=1024, tk=1408, res=x)
        else:
            e = l // 2
            w_r = jnp.pad(moe_router[e], ((0, 0), (0, 128 - N_EXPERTS)))
            b_r = row(jnp.pad(moe_router_b[e], (0, 128 - N_EXPERTS)))
            comb = router(x, g_ffn, w_r, b_r, expand, tm=tm)
            fe = moe_w_gate.shape[-1]
            wg = moe_w_gate[e].transpose(1, 0, 2).reshape(D_MODEL, N_EXPERTS * fe).astype(bf16)
            wu = moe_w_up[e].transpose(1, 0, 2).reshape(D_MODEL, N_EXPERTS * fe).astype(bf16)
            wd = moe_w_down[e].reshape(N_EXPERTS * fe, D_MODEL).astype(bf16)
            a = ffn1(x, g_ffn, wg, wu, tm=tm, tn=1408, comb=comb, per_expert=fe)
            x = fmm(a, wd, tm=tm, tn=1024, tk=fe, res=x)

    st = lambda k: jnp.stack(outs[k])
    return (x[:T][None], x[T:, None],
            st("pool_p"), st("lat_p"), st("rope_p"), st("rwkv_p"), st("shift_p"), st("memk_p"), st("memv_p"),
            st("pool_s"), st("lat_s"), st("rope_s"), st("rwkv_s"), st("shift_s"))
```

```python
import functools

import jax
import jax.numpy as jnp
from jax import lax
from jax.experimental import pallas as pl
from jax.experimental.pallas import tpu as pltpu

f32 = jnp.float32
bf16 = jnp.bfloat16

D_MODEL = 2048
MIX_W = 1024
POOL_WINDOWS = (2, 4, 8, 16)
POOL_GW = 256
POOL_HIST = 15
MLA_HEADS = 8
MLA_NOPE = 128
MLA_ROPE = 64
MLA_QK = MLA_NOPE + MLA_ROPE
MLA_Q_LORA = 512
MLA_KV_LORA = 256
MLA_SCALE = MLA_QK ** -0.5
ROPE_THETA = 10000.0
PAGE_SIZE = 128
RWKV_HEAD = 64
RWKV_HEADS = 16
RWKV_PROJ = 3328
MEM_HEADS = 4
MEM_HEAD_DIM = 128
MEM_W = 512
N_EXPERTS = 8
RMS_EPS = 1e-6
GN_EPS = 64e-5
NEG_INF = -1e30
OFF_CQ, OFF_CKV, OFF_KR, OFF_RWKV, OFF_GATE = 1024, 1536, 1792, 1856, 5184
HEAD_PAD = 256
HEAD_GROUP = 4
SCAN_CHUNK = 64
VMEM_LIMIT = 50 * 1024 * 1024
HI = lax.Precision.HIGHEST
SCAN_EXACT = False


def _cp(*sem, vmem=VMEM_LIMIT):
    return pltpu.CompilerParams(dimension_semantics=sem, vmem_limit_bytes=vmem)


def _tile(n, pref, mult=8):
    best = None
    t = mult
    while t <= min(n, pref):
        if n % t == 0:
            best = t
        t += mult
    return best if best is not None else n


def _nt(a, b, precision=None):
    return lax.dot_general(a, b, (((1,), (1,)), ((), ())), precision=precision,
                           preferred_element_type=f32)


def _mm(a, b, precision=None):
    return jnp.dot(a, b, precision=precision, preferred_element_type=f32)


def _rms(x, g):
    return x * lax.rsqrt(jnp.mean(x * x, axis=-1, keepdims=True) + RMS_EPS) * g


def _segsum(x, ind):
    hi = x.astype(bf16)
    lo = (x - hi.astype(f32)).astype(bf16)
    return _mm(hi, ind) + _mm(lo, ind)


def _fmm_norm_kernel(x_ref, g_ref, w_ref, *rest, has_res):
    if has_res:
        r_ref, o_ref, xn_ref = rest
    else:
        o_ref, xn_ref = rest

    @pl.when(pl.program_id(1) == 0)
    def _():
        xn_ref[...] = _rms(x_ref[...], g_ref[...]).astype(bf16)

    acc = _mm(xn_ref[...], w_ref[...])
    if has_res:
        acc = acc + r_ref[...]
    o_ref[...] = acc.astype(o_ref.dtype)


def fmm_norm(x, g, w, *, tm, tn, out_dtype=f32, res=None):
    M, K = x.shape
    N = w.shape[1]
    in_specs = [pl.BlockSpec((tm, K), lambda i, j: (i, 0)),
                pl.BlockSpec((1, K), lambda i, j: (0, 0)),
                pl.BlockSpec((K, tn), lambda i, j: (0, j))]
    args = [x, g, w]
    if res is not None:
        in_specs.append(pl.BlockSpec((tm, tn), lambda i, j: (i, j)))
        args.append(res)
    return pl.pallas_call(
        functools.partial(_fmm_norm_kernel, has_res=res is not None),
        grid=(M // tm, N // tn),
        in_specs=in_specs,
        out_specs=pl.BlockSpec((tm, tn), lambda i, j: (i, j)),
        out_shape=jax.ShapeDtypeStruct((M, N), out_dtype),
        scratch_shapes=[pltpu.VMEM((tm, K), bf16)],
        compiler_params=_cp("parallel", "arbitrary"),
        name="fmm_norm",
    )(*args)


def _fmm_kernel(x_ref, w_ref, *rest, has_res, nk):
    if has_res:
        r_ref, o_ref, acc_ref = rest
    else:
        o_ref, acc_ref = rest
    k = pl.program_id(2)

    @pl.when(k == 0)
    def _():
        acc_ref[...] = jnp.zeros_like(acc_ref)

    acc_ref[...] += _mm(x_ref[...].astype(bf16), w_ref[...])

    @pl.when(k == nk - 1)
    def _():
        a = acc_ref[...]
        if has_res:
            a = a + r_ref[...]
        o_ref[...] = a.astype(o_ref.dtype)


def fmm(x, w, *, tm, tn, tk, out_dtype=f32, res=None):
    M, K = x.shape
    N = w.shape[1]
    nk = K // tk
    in_specs = [pl.BlockSpec((tm, tk), lambda i, j, k: (i, k)),
                pl.BlockSpec((tk, tn), lambda i, j, k: (k, j))]
    args = [x, w]
    if res is not None:
        in_specs.append(pl.BlockSpec((tm, tn), lambda i, j, k: (i, j)))
        args.append(res)
    return pl.pallas_call(
        functools.partial(_fmm_kernel, has_res=res is not None, nk=nk),
        grid=(M // tm, N // tn, nk),
        in_specs=in_specs,
        out_specs=pl.BlockSpec((tm, tn), lambda i, j, k: (i, j)),
        out_shape=jax.ShapeDtypeStruct((M, N), out_dtype),
        scratch_shapes=[pltpu.VMEM((tm, tn), f32)],
        compiler_params=_cp("parallel", "parallel", "arbitrary"),
        name="fmm",
    )(*args)


def _gmm_kernel(x_ref, w_ref, o_ref):
    o_ref[...] = _mm(x_ref[...].astype(bf16), w_ref[...]).astype(o_ref.dtype)


def gmm(x, w, *, tm, out_dtype=f32):
    M = x.shape[0]
    G, K, N = w.shape
    return pl.pallas_call(
        _gmm_kernel,
        grid=(M // tm, G),
        in_specs=[pl.BlockSpec((tm, K), lambda i, g: (i, g)),
                  pl.BlockSpec((None, K, N), lambda i, g: (g, 0, 0))],
        out_specs=pl.BlockSpec((tm, N), lambda i, g: (i, g)),
        out_shape=jax.ShapeDtypeStruct((M, G * N), out_dtype),
        compiler_params=_cp("parallel", "arbitrary"),
        name="gmm",
    )(x, w)


def _pool_p_kernel(u_ref, halo_ref, w_ref, sc_ref, o_ref, ext_ref, *, tm):
    i = pl.program_id(0)
    ext_ref[0:16, :] = jnp.where(i > 0, halo_ref[...], 0.0)
    ext_ref[16:, :] = u_ref[...]
    pos = i * tm + lax.broadcasted_iota(jnp.int32, (tm, 1), 0)
    for g, w in enumerate(POOL_WINDOWS):
        cols = slice(g * POOL_GW, (g + 1) * POOL_GW)
        s = ext_ref[16:, cols]
        for d in range(1, w):
            s = s + ext_ref[pl.ds(16 - d, tm), cols]
        pooled = s / jnp.minimum(pos + 1, w).astype(f32)
        diff = (pooled - ext_ref[16:, cols]).astype(bf16)
        o_ref[:, cols] = (_mm(diff, w_ref[g]) * sc_ref[:, cols]).astype(o_ref.dtype)


def pool_prompt(proj_a, pool_w, scale, *, T, tm):
    return pl.pallas_call(
        functools.partial(_pool_p_kernel, tm=tm),
        grid=(T // tm,),
        in_specs=[pl.BlockSpec((tm, MIX_W), lambda i: (i, 0)),
                  pl.BlockSpec((16, MIX_W), lambda i: (jnp.maximum(i * (tm // 16) - 1, 0), 0)),
                  pl.BlockSpec((4, POOL_GW, POOL_GW), lambda i: (0, 0, 0)),
                  pl.BlockSpec((1, MIX_W), lambda i: (0, 0))],
        out_specs=pl.BlockSpec((tm, MIX_W), lambda i: (i, 0)),
        out_shape=jax.ShapeDtypeStruct((T, MIX_W), bf16),
        scratch_shapes=[pltpu.VMEM((tm + 16, MIX_W), f32)],
        compiler_params=_cp("parallel"),
        name="pool_prompt",
    )(proj_a, proj_a, pool_w, scale)


def _pool_s_kernel(h_ref, u_ref, w_ref, sc_ref, o_ref, *, counts):
    for g, w in enumerate(POOL_WINDOWS):
        cols = slice(g * POOL_GW, (g + 1) * POOL_GW)
        u = u_ref[:, cols]
        s = u
        for d in range(1, w):
            s = s + h_ref[POOL_HIST - d, :, cols]
        diff = (s / counts[g] - u).astype(bf16)
        o_ref[:, cols] = (_mm(diff, w_ref[g]) * sc_ref[:, cols]).astype(o_ref.dtype)


def pool_sample(hist_t, proj_a, pool_w, scale, *, T, B, past_len):
    counts = tuple(float(min(past_len + 1, w)) for w in POOL_WINDOWS)
    return pl.pallas_call(
        functools.partial(_pool_s_kernel, counts=counts),
        grid=(1,),
        in_specs=[pl.BlockSpec((POOL_HIST, B, MIX_W), lambda i: (0, 0, 0)),
                  pl.BlockSpec((B, MIX_W), lambda i: (T // B, 0)),
                  pl.BlockSpec((4, POOL_GW, POOL_GW), lambda i: (0, 0, 0)),
                  pl.BlockSpec((1, MIX_W), lambda i: (0, 0))],
        out_specs=pl.BlockSpec((B, MIX_W), lambda i: (0, 0)),
        out_shape=jax.ShapeDtypeStruct((B, MIX_W), bf16),
        compiler_params=_cp("arbitrary"),
        name="pool_sample",
    )(hist_t, proj_a, pool_w, scale)


def _rope_half(hi, coef):
    rs = lax.rsqrt(jnp.sum(hi * hi, axis=-1, keepdims=True) * (0.5 / MLA_ROPE) + RMS_EPS)
    t = hi * rs * coef
    r = t + pltpu.roll(t, 64, 1)
    lane = lax.broadcasted_iota(jnp.int32, (1, 128), 1)
    return jnp.where(lane < MLA_ROPE, r, 0.0)


def _qprep_kernel(c_ref, g_ref, w_ref, gq_ref, tab_ref, o_ref):
    cn = _rms(c_ref[...], g_ref[...]).astype(bf16)
    x = _mm(cn, w_ref[...])
    gq = gq_ref[...]
    coef = tab_ref[...] * gq[:, MLA_NOPE:]
    for h in range(MLA_HEADS):
        b = h * HEAD_PAD
        o_ref[:, b:b + MLA_NOPE] = _rms(x[:, b:b + MLA_NOPE], gq[:, :MLA_NOPE]).astype(o_ref.dtype)
        o_ref[:, b + MLA_NOPE:b + HEAD_PAD] = _rope_half(x[:, b + MLA_NOPE:b + HEAD_PAD], coef).astype(o_ref.dtype)


def qprep(proj_a, cq_g, w_big, gq, tab, *, tm):
    M = proj_a.shape[0]
    return pl.pallas_call(
        _qprep_kernel,
        grid=(M // tm,),
        in_specs=[pl.BlockSpec((tm, MLA_Q_LORA), lambda i: (i, OFF_CQ // MLA_Q_LORA)),
                  pl.BlockSpec((1, MLA_Q_LORA), lambda i: (0, 0)),
                  pl.BlockSpec((MLA_Q_LORA, MLA_HEADS * HEAD_PAD), lambda i: (0, 0)),
                  pl.BlockSpec((1, HEAD_PAD), lambda i: (0, 0)),
                  pl.BlockSpec((tm, 128), lambda i: (i, 0))],
        out_specs=pl.BlockSpec((tm, MLA_HEADS * HEAD_PAD), lambda i: (i, 0)),
        out_shape=jax.ShapeDtypeStruct((M, MLA_HEADS * HEAD_PAD), bf16),
        compiler_params=_cp("parallel"),
        name="qprep",
    )(proj_a, cq_g, w_big, gq, tab)


def _kprep_kernel(c_ref, g_ref, gk_ref, tab_ref, w_ref, kng_ref, lat_ref, latb_ref, kr_ref, k_ref):
    ck = c_ref[...]
    lat = _rms(ck[:, :MLA_KV_LORA], g_ref[...])
    lat_ref[...] = lat
    latb = lat.astype(bf16)
    latb_ref[...] = latb
    kr = _rope_half(ck[:, MLA_KV_LORA:], tab_ref[...] * gk_ref[...])
    kr_ref[...] = kr
    krb = kr.astype(bf16)
    kn = _mm(latb, w_ref[...])
    for h in range(MLA_HEADS):
        b = h * HEAD_PAD
        k_ref[:, b:b + MLA_NOPE] = _rms(kn[:, h * MLA_NOPE:(h + 1) * MLA_NOPE], kng_ref[...]).astype(bf16)
        k_ref[:, b + MLA_NOPE:b + HEAD_PAD] = krb


def kprep(proj_a, ckv_g, gk, tab, w_uk, kn_g, *, tm):
    M = proj_a.shape[0]
    wk = MLA_KV_LORA + 128
    row = lambda n: pl.BlockSpec((tm, n), lambda i: (i, 0))
    return pl.pallas_call(
        _kprep_kernel,
        grid=(M // tm,),
        in_specs=[pl.BlockSpec((tm, wk), lambda i: (i, OFF_CKV // wk)),
                  pl.BlockSpec((1, MLA_KV_LORA), lambda i: (0, 0)),
                  pl.BlockSpec((1, 128), lambda i: (0, 0)),
                  row(128),
                  pl.BlockSpec((MLA_KV_LORA, MLA_HEADS * MLA_NOPE), lambda i: (0, 0)),
                  pl.BlockSpec((1, MLA_NOPE), lambda i: (0, 0))],
        out_specs=[row(MLA_KV_LORA), row(MLA_KV_LORA), row(128), row(MLA_HEADS * HEAD_PAD)],
        out_shape=[jax.ShapeDtypeStruct((M, MLA_KV_LORA), f32),
                   jax.ShapeDtypeStruct((M, MLA_KV_LORA), bf16),
                   jax.ShapeDtypeStruct((M, 128), f32),
                   jax.ShapeDtypeStruct((M, MLA_HEADS * HEAD_PAD), bf16)],
        compiler_params=_cp("parallel"),
        name="kprep",
    )(proj_a, ckv_g, gk, tab, w_uk, kn_g)


def _attn_p_kernel(q_ref, k_ref, v_ref, o_ref, m_ref, l_ref, acc_ref, *, tq):
    qi = pl.program_id(0)
    ki = pl.program_id(1)

    @pl.when(ki == 0)
    def _():
        m_ref[...] = jnp.full_like(m_ref, NEG_INF)
        l_ref[...] = jnp.zeros_like(l_ref)
        acc_ref[...] = jnp.zeros_like(acc_ref)

    def step(masked):
        if masked:
            rows = lax.broadcasted_iota(jnp.int32, (tq, tq), 0)
            cols = lax.broadcasted_iota(jnp.int32, (tq, tq), 1)
            bias = jnp.where(cols <= rows, 0.0, NEG_INF)
        v = v_ref[...]
        rep = tq // 128
        for h0 in range(0, MLA_HEADS, HEAD_GROUP):
            H = range(h0, h0 + HEAD_GROUP)
            hs = {h: slice(h * HEAD_PAD, (h + 1) * HEAD_PAD) for h in H}
            s = {h: _nt(q_ref[:, hs[h]], k_ref[:, hs[h]]) for h in H}
            if masked:
                s = {h: s[h] + bias for h in H}
            m_prev = {h: m_ref[h] for h in H}
            m_new = {h: jnp.maximum(m_prev[h], jnp.max(s[h], axis=1, keepdims=True)) for h in H}
            p = {h: jnp.exp(s[h] - jnp.concatenate([m_new[h]] * rep, axis=1)) for h in H}
            alpha = {h: jnp.exp(m_prev[h] - m_new[h]) for h in H}
            pv = {h: _mm(p[h].astype(bf16), v) for h in H}
            for h in H:
                l_ref[h] = alpha[h] * l_ref[h] + jnp.sum(p[h], axis=1, keepdims=True)
                acc_ref[h] = jnp.concatenate([alpha[h]] * (MLA_KV_LORA // 128), axis=1) * acc_ref[h] + pv[h]
                m_ref[h] = m_new[h]

    @pl.when(ki < qi)
    def _():
        step(False)

    @pl.when(ki == qi)
    def _():
        step(True)

    @pl.when(ki == qi)
    def _():
        for h in range(MLA_HEADS):
            hs = slice(h * HEAD_PAD, (h + 1) * HEAD_PAD)
            o_ref[:, hs] = (acc_ref[h] / l_ref[h][:, :1]).astype(o_ref.dtype)


def attn_prompt(q_full, k_full, lat_b, *, T, tq):
    n = T // tq
    W = MLA_HEADS * HEAD_PAD
    return pl.pallas_call(
        functools.partial(_attn_p_kernel, tq=tq),
        grid=(n, n),
        in_specs=[pl.BlockSpec((tq, W), lambda qi, ki: (qi, 0)),
                  pl.BlockSpec((tq, W), lambda qi, ki: (jnp.minimum(ki, qi), 0)),
                  pl.BlockSpec((tq, MLA_KV_LORA), lambda qi, ki: (jnp.minimum(ki, qi), 0))],
        out_specs=pl.BlockSpec((tq, W), lambda qi, ki: (qi, 0)),
        out_shape=jax.ShapeDtypeStruct((T, W), bf16),
        scratch_shapes=[pltpu.VMEM((MLA_HEADS, tq, 128), f32),
                        pltpu.VMEM((MLA_HEADS, tq, 128), f32),
                        pltpu.VMEM((MLA_HEADS, tq, MLA_KV_LORA), f32)],
        compiler_params=_cp("parallel", "arbitrary"),
        name="attn_prompt",
    )(q_full, k_full, lat_b)


def _attn_s_kernel(pt_ref, qa_ref, qr_ref, qf_ref, kn_ref, ln_ref, wt_ref, *rest, G, ng):
    lat_refs = rest[:G]
    rope_refs = rest[G:2 * G]
    o_ref, m_ref, l_ref, acc_ref = rest[2 * G:]
    g = pl.program_id(1)

    @pl.when(g == 0)
    def _():
        m_ref[...] = jnp.full_like(m_ref, NEG_INF)
        l_ref[...] = jnp.zeros_like(l_ref)
        acc_ref[...] = jnp.zeros_like(acc_ref)

    qa = qa_ref[...]
    qr = qr_ref[...]
    wt = wt_ref[...]
    lat_all = jnp.concatenate([r[...].astype(bf16) for r in lat_refs], axis=0)
    pairs = range(0, G, 2)
    lat = [lat_all[i * PAGE_SIZE:(i + 2) * PAGE_SIZE] for i in pairs]
    rope_t = [jnp.concatenate([rope_refs[i][...], rope_refs[i + 1][...]], axis=1).astype(bf16) for i in pairs]
    kt = [_nt(wt, x) for x in lat]
    num = [_nt(qa, x) for x in lat]
    rr = [_mm(qr, x) for x in rope_t]
    ssq = [jnp.sum((x * x).reshape(MLA_HEADS, MLA_NOPE, 2 * PAGE_SIZE), axis=1) for x in kt]
    parts = [n * lax.rsqrt(q * (1.0 / MLA_NOPE) + RMS_EPS) + r for n, q, r in zip(num, ssq, rr)]
    s = jnp.concatenate(parts, axis=1)
    m_prev = m_ref[...]
    m_new = jnp.maximum(m_prev, jnp.max(s, axis=1, keepdims=True))
    alpha = jnp.exp(m_prev - m_new)
    p = jnp.exp(s - m_new[:, :1])
    l_ref[...] = alpha * l_ref[...] + jnp.sum(p, axis=1, keepdims=True)
    acc_ref[...] = alpha[:, :1] * acc_ref[...] + _mm(p.astype(bf16), lat_all)
    m_ref[...] = m_new

    @pl.when(g == ng - 1)
    def _():
        s_new = jnp.sum(qf_ref[...].astype(f32) * kn_ref[...].astype(f32), axis=1, keepdims=True)
        m_prev = m_ref[...]
        m_new = jnp.maximum(m_prev, s_new)
        alpha = jnp.exp(m_prev - m_new)
        p = jnp.exp(s_new - m_new)
        l = alpha * l_ref[...] + p
        acc = alpha[:, :1] * acc_ref[...] + p[:, :1] * ln_ref[...].astype(bf16).astype(f32)
        o_ref[...] = (acc / l[:, :1]).astype(o_ref.dtype)


def attn_sample(page_table, qa, qr, qf, knew, latnew, wuk_t, cache_lat, cache_rope_t, *, layer, G):
    B, n_pages = page_table.shape
    ng = n_pages // G
    pt = page_table.reshape(-1)
    per_b = lambda *shape: pl.BlockSpec((None,) + shape, lambda b, g, pt: (b,) + (0,) * len(shape))

    def page_spec(i, rows, width):
        return pl.BlockSpec((None, None, rows, width),
                            lambda b, g, pt: (layer, pt[b * n_pages + g * G + i], 0, 0))

    in_specs = [per_b(MLA_HEADS, MLA_KV_LORA), per_b(MLA_HEADS, MLA_ROPE), per_b(MLA_HEADS, HEAD_PAD),
                per_b(MLA_HEADS, HEAD_PAD), per_b(1, MLA_KV_LORA),
                pl.BlockSpec((MLA_HEADS * MLA_NOPE, MLA_KV_LORA), lambda b, g, pt: (0, 0))]
    in_specs += [page_spec(i, PAGE_SIZE, MLA_KV_LORA) for i in range(G)]
    in_specs += [page_spec(i, MLA_ROPE, PAGE_SIZE) for i in range(G)]
    return pl.pallas_call(
        functools.partial(_attn_s_kernel, G=G, ng=ng),
        grid_spec=pltpu.PrefetchScalarGridSpec(
            num_scalar_prefetch=1,
            grid=(B, ng),
            in_specs=in_specs,
            out_specs=per_b(MLA_HEADS, MLA_KV_LORA),
            scratch_shapes=[pltpu.VMEM((MLA_HEADS, 128), f32),
                            pltpu.VMEM((MLA_HEADS, 128), f32),
                            pltpu.VMEM((MLA_HEADS, MLA_KV_LORA), f32)]),
        out_shape=jax.ShapeDtypeStruct((B, MLA_HEADS, MLA_KV_LORA), bf16),
        compiler_params=_cp("parallel", "arbitrary"),
        name="attn_sample",
    )(pt, qa, qr, qf, knew, latnew, wuk_t, *([cache_lat] * G), *([cache_rope_t] * G))


def _rwkv_prep_math(z, zs, mu_ref, wcat_ref, w0_ref, a0_ref, kk_ref, ka_ref, rk_ref, ind_ref, outs):
    r_o, k_o, v_o, kk_o, b_o, lw_o, g_o, bonus_o = outs
    zm = z + (zs - z) * mu_ref[...]
    r = zm[:, :MIX_W]
    k = zm[:, MIX_W:2 * MIX_W]
    v = zm[:, 2 * MIX_W:3 * MIX_W]
    tail = zm[:, 3 * MIX_W:]
    lane = lax.broadcasted_iota(jnp.int32, (1, 256), 1)
    act = jnp.where(lane < 64, jnp.tanh(tail), jnp.where(lane < 128, tail, 1.0 / (1.0 + jnp.exp(-tail))))
    lo = _mm(act.astype(bf16), wcat_ref[...])
    u = -(w0_ref[...] + lo[:, :MIX_W])
    w = -(jnp.maximum(u, 0.0) + jnp.log(1.0 + jnp.exp(-jnp.abs(u)))) - 0.5
    a = 1.0 / (1.0 + jnp.exp(-(a0_ref[...] + lo[:, MIX_W:2 * MIX_W])))
    ind = ind_ref[...]
    kk = k * kk_ref[...]
    kk = kk * lax.rsqrt(_segsum(kk * kk, ind) + 1e-12)
    k_eff = k * (1.0 + (a - 1.0) * ka_ref[...])
    r_o[...] = r
    k_o[...] = k_eff
    v_o[...] = v
    kk_o[...] = kk
    b_o[...] = kk * a
    lw_o[...] = -jnp.exp(w)
    g_o[...] = lo[:, 2 * MIX_W:]
    bonus_o[...] = _segsum(r * k_eff * rk_ref[...], ind) * v


def _rwkv_prep_p_kernel(z_ref, halo_ref, *rest):
    params, outs = rest[:8], rest[8:]
    z = z_ref[...]
    first = pl.program_id(0) == 0
    prev = jnp.where(first, 0.0, halo_ref[7:8, :])
    row = lax.broadcasted_iota(jnp.int32, (z.shape[0], 1), 0)
    zs = jnp.where(row == 0, prev, pltpu.roll(z, 1, 0))
    _rwkv_prep_math(z, zs, *params, outs)


def _rwkv_prep_s_kernel(z_ref, zs_ref, *rest):
    params, outs = rest[:8], rest[8:]
    _rwkv_prep_math(z_ref[...], zs_ref[...], *params, outs)


def rwkv_prep(proj_z, shift, params, *, T, B, tm):
    const = lambda a: pl.BlockSpec(a.shape, lambda i: (0,) * a.ndim)
    pspecs = [const(p) for p in params]
    outs_p = [jax.ShapeDtypeStruct((T, MIX_W), f32)] * 8
    res_p = pl.pallas_call(
        _rwkv_prep_p_kernel,
        grid=(T // tm,),
        in_specs=[pl.BlockSpec((tm, RWKV_PROJ), lambda i: (i, 0)),
                  pl.BlockSpec((8, RWKV_PROJ), lambda i: (jnp.maximum(i * (tm // 8) - 1, 0), 0))] + pspecs,
        out_specs=[pl.BlockSpec((tm, MIX_W), lambda i: (i, 0))] * 8,
        out_shape=outs_p,
        compiler_params=_cp("parallel"),
        name="rwkv_prep_prompt",
    )(proj_z, proj_z, *params)
    res_s = pl.pallas_call(
        _rwkv_prep_s_kernel,
        grid=(1,),
        in_specs=[pl.BlockSpec((B, RWKV_PROJ), lambda i: (T // B, 0)),
                  pl.BlockSpec((B, RWKV_PROJ), lambda i: (0, 0))] + pspecs,
        out_specs=[pl.BlockSpec((B, MIX_W), lambda i: (0, 0))] * 8,
        out_shape=[jax.ShapeDtypeStruct((B, MIX_W), f32)] * 8,
        compiler_params=_cp("arbitrary"),
        name="rwkv_prep_sample",
    )(proj_z, shift, *params)
    return res_p, res_s


def _scan_kernel(r_ref, k_ref, v_ref, kk_ref, b_ref, lw_ref, lwt_ref, bt_ref, kt_ref,
                 y_ref, so_ref, st_ref, *, exact):
    C = SCAN_CHUNK
    N = RWKV_HEAD

    @pl.when(pl.program_id(0) == 0)
    def _():
        st_ref[...] = jnp.zeros_like(st_ref)

    row = lax.broadcasted_iota(jnp.int32, (C, C), 0)
    col = lax.broadcasted_iota(jnp.int32, (C, C), 1)
    strict = col < row
    incl = col <= row
    eye = (row == col).astype(f32)
    if exact:
        op = lambda x: x
        mm = functools.partial(_mm, precision=HI)
        nt = functools.partial(_nt, precision=HI)
    else:
        op = lambda x: x.astype(bf16)
        mm, nt = _mm, _nt

    lw = lw_ref[...]
    cs = _mm(incl.astype(f32), lw, precision=HI)
    cst = _mm(lwt_ref[...], (row <= col).astype(f32), precision=HI)
    em = jnp.exp(-cs)
    emt = jnp.exp(-cst)
    ar_all = op(jnp.concatenate([-kk_ref[...] * jnp.exp(cs - lw), r_ref[...] * jnp.exp(cs)], axis=0))
    bh_all = op(b_ref[...] * em)
    kh_all = op(k_ref[...] * em)
    bkt_all = op(jnp.concatenate([bt_ref[...] * emt, kt_ref[...] * emt], axis=1))
    g_all = jnp.exp(cst[:, C - 1:C])
    v_all = op(v_ref[...])
    H = range(RWKV_HEADS)
    hsl = [slice(h * N, (h + 1) * N) for h in H]
    ar = [ar_all[:, s] for s in hsl]
    v = [v_all[:, s] for s in hsl]
    pb = [nt(ar[h], bh_all[:, hsl[h]]) for h in H]
    pk = [nt(ar[h], kh_all[:, hsl[h]]) for h in H]
    pw = [jnp.where(strict, pb[h][:C], 0.0) for h in H]
    tinv = [eye + pw[h] for h in H]
    for _ in range(5):
        pw = [mm(op(pw[h]), op(pw[h])) for h in H]
        tinv = [tinv[h] + mm(op(tinv[h]), op(pw[h])) for h in H]
    st = [st_ref[h] for h in H]
    xs = [mm(ar[h], op(st[h])) for h in H]
    pv = [mm(op(jnp.concatenate([jnp.where(strict, pk[h][:C], 0.0), jnp.where(incl, pk[h][C:], 0.0)], axis=0)), v[h])
          for h in H]
    ut = [mm(op(tinv[h]), op(xs[h][:C] + pv[h][:C])) for h in H]
    for h in H:
        y_ref[:, hsl[h]] = xs[h][C:] + pv[h][C:] + mm(op(jnp.where(incl, pb[h][C:], 0.0)), op(ut[h]))
    for h in H:
        uv = jnp.concatenate([op(ut[h]), v[h]], axis=0)
        st_ref[h] = g_all[hsl[h]] * (st[h] + mm(bkt_all[hsl[h]], uv))
    so_ref[...] = st_ref[...]


def rwkv_scan(r, k, v, kk, b, lw, *, exact):
    T = r.shape[0]
    C = SCAN_CHUNK
    nc = T // C
    tr = lambda x: x.reshape(nc, C, MIX_W).transpose(0, 2, 1)
    row = pl.BlockSpec((C, MIX_W), lambda c: (c, 0))
    colm = pl.BlockSpec((None, MIX_W, C), lambda c: (c, 0, 0))
    return pl.pallas_call(
        functools.partial(_scan_kernel, exact=exact),
        grid=(nc,),
        in_specs=[row] * 6 + [colm] * 3,
        out_specs=[row, pl.BlockSpec((RWKV_HEADS, RWKV_HEAD, RWKV_HEAD), lambda c: (0, 0, 0))],
        out_shape=[jax.ShapeDtypeStruct((T, MIX_W), f32),
                   jax.ShapeDtypeStruct((RWKV_HEADS, RWKV_HEAD, RWKV_HEAD), f32)],
        scratch_shapes=[pltpu.VMEM((RWKV_HEADS, RWKV_HEAD, RWKV_HEAD), f32)],
        compiler_params=_cp("arbitrary"),
        name="rwkv_scan",
    )(r, k, v, kk, b, lw, tr(lw), tr(b), tr(k))


def _rwkv_step_kernel(s_ref, r_ref, k_ref, v_ref, kk_ref, b_ref, lw_ref, so_ref, y_ref):
    N = RWKV_HEAD
    eye = (lax.broadcasted_iota(jnp.int32, (N, N), 0) == lax.broadcasted_iota(jnp.int32, (N, N), 1)).astype(f32)
    s = s_ref[...]
    sa = jnp.sum(s * (-kk_ref[...]), axis=-1, keepdims=True)
    vcol = jnp.sum(eye * v_ref[...], axis=-1, keepdims=True)
    s_new = s * jnp.exp(lw_ref[...]) + sa * b_ref[...] + vcol * k_ref[...]
    so_ref[...] = s_new
    ycol = jnp.sum(s_new * r_ref[...], axis=-1, keepdims=True)
    y_ref[...] = jnp.sum(ycol * eye, axis=-2, keepdims=True)


def rwkv_step(state, r, k, v, kk, b, lw, *, tb):
    B = state.shape[0]
    H, N = RWKV_HEADS, RWKV_HEAD
    vec = lambda x: x.reshape(B, H, 1, N)
    vspec = pl.BlockSpec((tb, H, 1, N), lambda i: (i, 0, 0, 0))
    sspec = pl.BlockSpec((tb, H, N, N), lambda i: (i, 0, 0, 0))
    s_new, y = pl.pallas_call(
        _rwkv_step_kernel,
        grid=(B // tb,),
        in_specs=[sspec] + [vspec] * 6,
        out_specs=[sspec, vspec],
        out_shape=[jax.ShapeDtypeStruct((B, H, N, N), f32), jax.ShapeDtypeStruct((B, H, 1, N), f32)],
        compiler_params=_cp("parallel"),
        name="rwkv_step",
    )(state, vec(r), vec(k), vec(v), vec(kk), vec(b), vec(lw))
    return s_new, y.reshape(B, MIX_W)


def _rwkv_post_kernel(y_ref, bonus_ref, g_ref, lnw_ref, lnb_ref, ind_ref, o_ref):
    ind = ind_ref[...]
    y = y_ref[...]
    d = y - _segsum(y, ind) * (1.0 / RWKV_HEAD)
    var = _segsum(d * d, ind) * (1.0 / RWKV_HEAD)
    yn = d * lax.rsqrt(var + GN_EPS) * lnw_ref[...] + lnb_ref[...]
    o_ref[...] = ((yn + bonus_ref[...]) * g_ref[...]).astype(o_ref.dtype)


def rwkv_post(y, bonus, g, ln_w, ln_b, ind, *, tm):
    M = y.shape[0]
    row = pl.BlockSpec((tm, MIX_W), lambda i: (i, 0))
    vec = pl.BlockSpec((1, MIX_W), lambda i: (0, 0))
    return pl.pallas_call(
        _rwkv_post_kernel,
        grid=(M // tm,),
        in_specs=[row, row, row, vec, vec, pl.BlockSpec((MIX_W, MIX_W), lambda i: (0, 0))],
        out_specs=row,
        out_shape=jax.ShapeDtypeStruct((M, MIX_W), bf16),
        compiler_params=_cp("parallel"),
        name="rwkv_post",
    )(y, bonus, g, ln_w, ln_b, ind)


def _merge_kernel(b0_ref, b1_ref, b2_ref, w_ref, g0_ref, g1_ref, g2_ref, o_ref):
    acc = None
    for n, (b_ref, g_ref) in enumerate(((b0_ref, g0_ref), (b1_ref, g1_ref), (b2_ref, g2_ref))):
        gate = 1.0 / (1.0 + jnp.exp(-g_ref[...]))
        t = gate * _mm(b_ref[...], w_ref[n])
        acc = t if acc is None else acc + t
    o_ref[...] = acc.astype(o_ref.dtype)


def merge(branches, w_branch, proj_g, *, tm, tn):
    M = proj_g.shape[0]
    nb = D_MODEL // tn
    bspec = pl.BlockSpec((tm, MIX_W), lambda i, j: (i, 0))
    gspec = lambda n: pl.BlockSpec((tm, tn), lambda i, j: (i, n * nb + j))
    return pl.pallas_call(
        _merge_kernel,
        grid=(M // tm, nb),
        in_specs=[bspec, bspec, bspec, pl.BlockSpec((3, MIX_W, tn), lambda i, j: (0, 0, j)),
                  gspec(0), gspec(1), gspec(2)],
        out_specs=pl.BlockSpec((tm, tn), lambda i, j: (i, j)),
        out_shape=jax.ShapeDtypeStruct((M, D_MODEL), bf16),
        compiler_params=_cp("parallel", "arbitrary"),
        name="merge",
    )(*branches, w_branch, proj_g, proj_g, proj_g)


def _memkv_kernel(kv_ref, g_ref, k_ref, v_ref):
    for h in range(MEM_HEADS):
        hs = slice(h * MEM_HEAD_DIM, (h + 1) * MEM_HEAD_DIM)
        k_ref[:, hs] = _rms(kv_ref[:, hs], g_ref[...])
    v_ref[...] = kv_ref[:, MEM_W:]


def memkv_post(kv, g):
    n = kv.shape[0]
    return pl.pallas_call(
        _memkv_kernel,
        grid=(1,),
        in_specs=[pl.BlockSpec((n, 2 * MEM_W), lambda i: (0, 0)), pl.BlockSpec((1, MEM_HEAD_DIM), lambda i: (0, 0))],
        out_specs=[pl.BlockSpec((n, MEM_W), lambda i: (0, 0))] * 2,
        out_shape=[jax.ShapeDtypeStruct((n, MEM_W), f32)] * 2,
        compiler_params=_cp("arbitrary"),
        name="memkv_post",
    )(kv, g)


def _memattn_p_kernel(q_ref, g_ref, k_ref, v_ref, o_ref):
    for h in range(MEM_HEADS):
        hs = slice(h * MEM_HEAD_DIM, (h + 1) * MEM_HEAD_DIM)
        q = _rms(q_ref[:, hs], g_ref[...]).astype(bf16)
        s = _nt(q, k_ref[:, hs].astype(bf16)) * (MEM_HEAD_DIM ** -0.5)
        p = jnp.exp(s - jnp.max(s, axis=1, keepdims=True))
        o = _mm(p.astype(bf16), v_ref[:, hs].astype(bf16)) / jnp.sum(p, axis=1, keepdims=True)
        o_ref[:, hs] = o.astype(o_ref.dtype)


def memattn_prompt(q, g, mem_k, mem_v, *, T, tm):
    n = mem_k.shape[0]
    return pl.pallas_call(
        _memattn_p_kernel,
        grid=(T // tm,),
        in_specs=[pl.BlockSpec((tm, MEM_W), lambda i: (i, 0)), pl.BlockSpec((1, MEM_HEAD_DIM), lambda i: (0, 0)),
                  pl.BlockSpec((n, MEM_W), lambda i: (0, 0)), pl.BlockSpec((n, MEM_W), lambda i: (0, 0))],
        out_specs=pl.BlockSpec((tm, MEM_W), lambda i: (i, 0)),
        out_shape=jax.ShapeDtypeStruct((T, MEM_W), bf16),
        compiler_params=_cp("parallel"),
        name="memattn_prompt",
    )(q, g, mem_k, mem_v)


def _memattn_s_kernel(q_ref, g_ref, k_ref, v_ref, o_ref):
    for h in range(MEM_HEADS):
        hs = slice(h * MEM_HEAD_DIM, (h + 1) * MEM_HEAD_DIM)
        q = _rms(q_ref[:, :, hs], g_ref[...])
        s = jnp.sum(k_ref[:, :, hs] * q, axis=-1, keepdims=True) * (MEM_HEAD_DIM ** -0.5)
        p = jnp.exp(s - jnp.max(s, axis=1, keepdims=True))
        o = jnp.sum(p * v_ref[:, :, hs], axis=1, keepdims=True) / jnp.sum(p, axis=1, keepdims=True)
        o_ref[:, :, hs] = o.astype(o_ref.dtype)


def memattn_sample(q, g, cache_k, cache_v, *, layer, T, B, tb):
    n = cache_k.shape[2]
    q3 = q.reshape(q.shape[0], 1, MEM_W)
    ck = cache_k.reshape(cache_k.shape[0], B, n, MEM_W)
    cv = cache_v.reshape(cache_v.shape[0], B, n, MEM_W)
    kvspec = pl.BlockSpec((None, tb, n, MEM_W), lambda i: (layer, i, 0, 0))
    out = pl.pallas_call(
        _memattn_s_kernel,
        grid=(B // tb,),
        in_specs=[pl.BlockSpec((tb, 1, MEM_W), lambda i: (T // tb + i, 0, 0)),
                  pl.BlockSpec((1, MEM_HEAD_DIM), lambda i: (0, 0)), kvspec, kvspec],
        out_specs=pl.BlockSpec((tb, 1, MEM_W), lambda i: (i, 0, 0)),
        out_shape=jax.ShapeDtypeStruct((B, 1, MEM_W), bf16),
        compiler_params=_cp("parallel"),
        name="memattn_sample",
    )(q3, g, ck, cv)
    return out.reshape(B, MEM_W)


def _ffn1_kernel(x_ref, g_ref, wg_ref, wu_ref, *rest, scaled):
    if scaled:
        c_ref, o_ref, xn_ref = rest
    else:
        o_ref, xn_ref = rest

    @pl.when(pl.program_id(1) == 0)
    def _():
        xn_ref[...] = _rms(x_ref[...], g_ref[...]).astype(bf16)

    xn = xn_ref[...]
    a = _mm(xn, wg_ref[...])
    a = a / (1.0 + jnp.exp(-a)) * _mm(xn, wu_ref[...])
    if scaled:
        a = a * c_ref[:, :1]
    o_ref[...] = a.astype(o_ref.dtype)


def ffn1(x, g, wg, wu, *, tm, tn, comb=None, per_expert=None):
    M, K = x.shape
    F = wg.shape[1]
    in_specs = [pl.BlockSpec((tm, K), lambda i, j: (i, 0)), pl.BlockSpec((1, K), lambda i, j: (0, 0)),
                pl.BlockSpec((K, tn), lambda i, j: (0, j)), pl.BlockSpec((K, tn), lambda i, j: (0, j))]
    args = [x, g, wg, wu]
    if comb is not None:
        nper = per_expert // tn
        in_specs.append(pl.BlockSpec((tm, 128), lambda i, j: (i, j // nper)))
        args.append(comb)
    return pl.pallas_call(
        functools.partial(_ffn1_kernel, scaled=comb is not None),
        grid=(M // tm, F // tn),
        in_specs=in_specs,
        out_specs=pl.BlockSpec((tm, tn), lambda i, j: (i, j)),
        out_shape=jax.ShapeDtypeStruct((M, F), bf16),
        scratch_shapes=[pltpu.VMEM((tm, K), bf16)],
        compiler_params=_cp("parallel", "arbitrary"),
        name="ffn1",
    )(*args)


def _router_kernel(x_ref, g_ref, w_ref, b_ref, e_ref, o_ref):
    logits = _mm(_rms(x_ref[...], g_ref[...]), w_ref[...], precision=HI) + b_ref[...]
    lane = lax.broadcasted_iota(jnp.int32, logits.shape, 1)
    logits = jnp.where(lane < N_EXPERTS, logits, -jnp.inf)
    m1 = jnp.max(logits, axis=1, keepdims=True)
    i1 = jnp.min(jnp.where(logits == m1, lane, 128), axis=1, keepdims=True)
    rest = jnp.where(lane == i1, -jnp.inf, logits)
    m2 = jnp.max(rest, axis=1, keepdims=True)
    i2 = jnp.min(jnp.where(rest == m2, lane, 128), axis=1, keepdims=True)
    e2 = jnp.exp(m2 - m1)
    w1 = 1.0 / (1.0 + e2)
    w2 = e2 / (1.0 + e2)
    comb = jnp.where(lane == i1, w1, 0.0) + jnp.where(lane == i2, w2, 0.0)
    o_ref[...] = _mm(comb, e_ref[...], precision=HI)


def router(x, g, w_pad, b_pad, expand, *, tm):
    M, K = x.shape
    return pl.pallas_call(
        _router_kernel,
        grid=(M // tm,),
        in_specs=[pl.BlockSpec((tm, K), lambda i: (i, 0)), pl.BlockSpec((1, K), lambda i: (0, 0)),
                  pl.BlockSpec((K, 128), lambda i: (0, 0)), pl.BlockSpec((1, 128), lambda i: (0, 0)),
                  pl.BlockSpec((128, N_EXPERTS * 128), lambda i: (0, 0))],
        out_specs=pl.BlockSpec((tm, N_EXPERTS * 128), lambda i: (i, 0)),
        out_shape=jax.ShapeDtypeStruct((M, N_EXPERTS * 128), f32),
        compiler_params=_cp("parallel"),
        name="router",
    )(x, g, w_pad, b_pad, expand)


def _swap_halves(x):
    h = x.shape[-1] // 2
    return jnp.concatenate([x[..., h:], x[..., :h]], axis=-1)


def kernel(x_prompt, x_sample, mem_prompt, state_pool, cache_mla_latent, cache_mla_rope, state_rwkv, state_rwkv_shift, cache_mem_k, cache_mem_v, page_table, norm_mix_g, w_in, pool_w, pool_scale, mla_cq_g, mla_w_uq, mla_ckv_g, mla_kr_g, mla_w_uk, mla_w_uv, mla_qn_g, mla_qr_g, mla_kn_g, rwkv_mu, rwkv_w0, rwkv_w2, rwkv_a0, rwkv_a2, rwkv_g2, rwkv_k_k, rwkv_k_a, rwkv_r_k, rwkv_ln_w, rwkv_ln_b, w_branch, w_out, norm_mem_g, mem_norm_g, w_q_mem, w_k_mem, w_v_mem, mem_qn_g, mem_kn_g, w_o_mem, norm_ffn_g, ffn_w_gate, ffn_w_up, ffn_w_down, moe_router, moe_router_b, moe_w_gate, moe_w_up, moe_w_down):
    depth = w_in.shape[0]
    T = x_prompt.shape[1]
    B = x_sample.shape[0]
    assert x_prompt.shape[0] == 1 and x_sample.shape[1] == 1
    n_pages = page_table.shape[1]
    past_len = n_pages * PAGE_SIZE
    M = T + B
    assert T % B == 0 and T % SCAN_CHUNK == 0 and B % 8 == 0

    tm = _tile(M, 640)
    tp = _tile(T, 512)
    tq = _tile(T, 512, 128)
    tb = _tile(B, 8)
    G = _tile(n_pages, 16, 2)
    row = lambda v: v.reshape(1, -1)

    x = jnp.concatenate([x_prompt[0], x_sample[:, 0]], axis=0)
    cache_rope_t = jnp.swapaxes(cache_mla_rope, 2, 3)

    pos = jnp.concatenate([jnp.arange(T, dtype=jnp.int32), jnp.full((B,), past_len, jnp.int32)])
    inv = ROPE_THETA ** (-jnp.arange(0, MLA_ROPE, 2, dtype=f32) / MLA_ROPE)
    ang = pos.astype(f32)[:, None] * inv[None, :]
    cos, sin = jnp.cos(ang), jnp.sin(ang)
    tab = jnp.concatenate([cos, cos, -sin, sin], axis=1)

    ids = jnp.arange(MIX_W) // RWKV_HEAD
    ind = (ids[:, None] == ids[None, :]).astype(bf16)
    expand = (jnp.arange(128)[:, None] == (jnp.arange(N_EXPERTS * 128) // 128)[None, :]).astype(f32)

    outs = {k: [] for k in ("pool_p", "lat_p", "rope_p", "rwkv_p", "shift_p", "memk_p", "memv_p",
                            "pool_s", "lat_s", "rope_s", "rwkv_s", "shift_s")}
    for l in range(depth):
        wi = w_in[l]
        kr_cols = wi[:, OFF_KR:OFF_RWKV]
        w_a = jnp.concatenate([wi[:, :OFF_KR], kr_cols, _swap_halves(kr_cols)], axis=1).astype(bf16)
        w_z = wi[:, OFF_RWKV:OFF_GATE].astype(bf16)
        w_g = wi[:, OFF_GATE:].astype(bf16)
        uq = mla_w_uq[l].reshape(MLA_Q_LORA, MLA_HEADS, MLA_QK)
        w_big = jnp.concatenate([uq, _swap_halves(uq[..., MLA_NOPE:])], axis=-1)
        w_big = w_big.reshape(MLA_Q_LORA, MLA_HEADS * HEAD_PAD).astype(bf16)
        gq = row(jnp.concatenate([mla_qn_g[l], mla_qr_g[l], _swap_halves(mla_qr_g[l])]) * MLA_SCALE)
        gk = row(jnp.concatenate([mla_kr_g[l], _swap_halves(mla_kr_g[l])]))
        w_uk = mla_w_uk[l].reshape(MLA_KV_LORA, MLA_HEADS * MLA_NOPE)
        w_uk_b = w_uk.astype(bf16)
        wuk_t = w_uk.T.astype(bf16)
        w_abs = (mla_w_uk[l] * mla_kn_g[l][None, None, :]).transpose(1, 2, 0)
        w_abs = jnp.concatenate([w_abs, jnp.zeros((MLA_HEADS, HEAD_PAD - MLA_NOPE, MLA_KV_LORA), f32)], axis=1).astype(bf16)
        w_uv = mla_w_uv[l].transpose(1, 0, 2).astype(bf16)
        wcat = jnp.zeros((256, 3 * MIX_W), f32)
        wcat = wcat.at[:64, :MIX_W].set(rwkv_w2[l]).at[64:128, MIX_W:2 * MIX_W].set(rwkv_a2[l])
        wcat = wcat.at[128:, 2 * MIX_W:].set(rwkv_g2[l]).astype(bf16)
        rparams = [row(rwkv_mu[l]), wcat, row(rwkv_w0[l]), row(rwkv_a0[l]), row(rwkv_k_k[l]), row(rwkv_k_a[l]),
                   row(rwkv_r_k[l]), ind]

        g_mix = row(norm_mix_g[l])
        proj_a = fmm_norm(x, g_mix, w_a, tm=tm, tn=w_a.shape[1])
        proj_z = fmm_norm(x, g_mix, w_z, tm=tm, tn=RWKV_PROJ // 2)
        proj_g = fmm_norm(x, g_mix, w_g, tm=tm, tn=1536)

        pw = pool_w[l].astype(bf16)
        o_pool = jnp.concatenate([
            pool_prompt(proj_a, pw, row(pool_scale[l]), T=T, tm=tp),
            pool_sample(state_pool[l].transpose(1, 0, 2), proj_a, pw, row(pool_scale[l]), T=T, B=B, past_len=past_len)])
        u = proj_a[:, :MIX_W]
        outs["pool_p"].append(u[T - POOL_HIST:T][None])
        outs["pool_s"].append(jnp.concatenate([state_pool[l][:, 1:], u[T:, None]], axis=1))

        q_full = qprep(proj_a, row(mla_cq_g[l]), w_big, gq, tab, tm=tm)
        lat, lat_b, krope, k_full = kprep(proj_a, row(mla_ckv_g[l]), gk, tab, w_uk_b, row(mla_kn_g[l]), tm=tm)
        outs["lat_p"].append(lat[:T][None])
        outs["lat_s"].append(lat[T:, None])
        outs["rope_p"].append(krope[:T, :MLA_ROPE][None])
        outs["rope_s"].append(krope[T:, None, :MLA_ROPE])
        o_lat_p = attn_prompt(q_full, k_full, lat_b, T=T, tq=tq)
        q_s = q_full[T:]
        qa = gmm(q_s, w_abs, tm=B, out_dtype=bf16).reshape(B, MLA_HEADS, MLA_KV_LORA)
        q_s3 = q_s.reshape(B, MLA_HEADS, HEAD_PAD)
        o_lat_s = attn_sample(page_table, qa, q_s3[:, :, MLA_NOPE:MLA_QK], q_s3,
                              k_full[T:].reshape(B, MLA_HEADS, HEAD_PAD), lat[T:, None], wuk_t,
                              cache_mla_latent, cache_rope_t, layer=l, G=G)
        o_lat = jnp.concatenate([o_lat_p, o_lat_s.reshape(B, MLA_HEADS * MLA_KV_LORA)])
        o_mla = gmm(o_lat, w_uv, tm=tm, out_dtype=bf16)

        (rp, rs_) = rwkv_prep(proj_z, state_rwkv_shift[l], rparams, T=T, B=B, tm=_tile(T, 256))
        y_p, st_t = rwkv_scan(rp[0], rp[1], rp[2], rp[3], rp[4], rp[5], exact=SCAN_EXACT)
        s_new, y_s = rwkv_step(state_rwkv[l], rs_[0], rs_[1], rs_[2], rs_[3], rs_[4], rs_[5], tb=tb)
        outs["rwkv_p"].append(st_t.transpose(0, 2, 1)[None])
        outs["rwkv_s"].append(s_new)
        outs["shift_p"].append(proj_z[T - 1:T])
        outs["shift_s"].append(proj_z[T:])
        cat = lambda a, b: jnp.concatenate([a, b])
        o_rwkv = rwkv_post(cat(y_p, y_s), cat(rp[7], rs_[7]), cat(rp[6], rs_[6]),
                           row(rwkv_ln_w[l]), row(rwkv_ln_b[l]), ind, tm=tm)

        merged = merge((o_pool, o_mla, o_rwkv), w_branch[l].astype(bf16), proj_g, tm=tm, tn=512)
        x = fmm(merged, w_out[l].astype(bf16), tm=tm, tn=1024, tk=D_MODEL, res=x)

        w_kv = jnp.concatenate([w_k_mem[l], w_v_mem[l]], axis=1).astype(bf16)
        n_mem = mem_prompt.shape[1]
        kv = fmm_norm(mem_prompt[0], row(mem_norm_g[l]), w_kv, tm=n_mem, tn=2 * MEM_W)
        mem_k, mem_v = memkv_post(kv, row(mem_kn_g[l]))
        outs["memk_p"].append(mem_k.reshape(1, n_mem, MEM_HEADS, MEM_HEAD_DIM))
        outs["memv_p"].append(mem_v.reshape(1, n_mem, MEM_HEADS, MEM_HEAD_DIM))
        q_mem = fmm_norm(x, row(norm_mem_g[l]), w_q_mem[l].astype(bf16), tm=tm, tn=MEM_W)
        o_mem = jnp.concatenate([
            memattn_prompt(q_mem, row(mem_qn_g[l]), mem_k, mem_v, T=T, tm=tp),
            memattn_sample(q_mem, row(mem_qn_g[l]), cache_mem_k, cache_mem_v, layer=l, T=T, B=B, tb=tb)])
        x = fmm(o_mem, w_o_mem[l].astype(bf16), tm=tm, tn=1024, tk=MEM_W, res=x)

        g_ffn = row(norm_ffn_g[l])
        if l % 2 == 0:
            d = l // 2
            a = ffn1(x, g_ffn, ffn_w_gate[d].astype(bf16), ffn_w_up[d].astype(bf16), tm=tm, tn=1408)
            x = fmm(a, ffn_w_down[d].astype(bf16), tm=tm, tn=1024, tk=1408, res=x)
        else:
            e = l // 2
            w_r = jnp.pad(moe_router[e], ((0, 0), (0, 128 - N_EXPERTS)))
            b_r = row(jnp.pad(moe_router_b[e], (0, 128 - N_EXPERTS)))
            comb = router(x, g_ffn, w_r, b_r, expand, tm=tm)
            fe = moe_w_gate.shape[-1]
            wg = moe_w_gate[e].transpose(1, 0, 2).reshape(D_MODEL, N_EXPERTS * fe).astype(bf16)
            wu = moe_w_up[e].transpose(1, 0, 2).reshape(D_MODEL, N_EXPERTS * fe).astype(bf16)
            wd = moe_w_down[e].reshape(N_EXPERTS * fe, D_MODEL).astype(bf16)
            a = ffn1(x, g_ffn, wg, wu, tm=tm, tn=1408, comb=comb, per_expert=fe)
            x = fmm(a, wd, tm=tm, tn=1024, tk=fe, res=x)

    st = lambda k: jnp.stack(outs[k])
    return (x[:T][None], x[T:, None],
            st("pool_p"), st("lat_p"), st("rope_p"), st("rwkv_p"), st("shift_p"), st("memk_p"), st("memv_p"),
            st("pool_s"), st("lat_s"), st("rope_s"), st("rwkv_s"), st("shift_s"))
```

```python
import functools

import jax
import jax.numpy as jnp
from jax import lax
from jax.experimental import pallas as pl
from jax.experimental.pallas import tpu as pltpu

f32 = jnp.float32
bf16 = jnp.bfloat16

D_MODEL = 2048
MIX_W = 1024
POOL_WINDOWS = (2, 4, 8, 16)
POOL_GW = 256
POOL_HIST = 15
MLA_HEADS = 8
MLA_NOPE = 128
MLA_ROPE = 64
MLA_QK = MLA_NOPE + MLA_ROPE
MLA_Q_LORA = 512
MLA_KV_LORA = 256
MLA_SCALE = MLA_QK ** -0.5
ROPE_THETA = 10000.0
PAGE_SIZE = 128
RWKV_HEAD = 64
RWKV_HEADS = 16
RWKV_PROJ = 3328
MEM_HEADS = 4
MEM_HEAD_DIM = 128
MEM_W = 512
N_EXPERTS = 8
RMS_EPS = 1e-6
GN_EPS = 64e-5
NEG_INF = -1e30
OFF_CQ, OFF_CKV, OFF_KR, OFF_RWKV, OFF_GATE = 1024, 1536, 1792, 1856, 5184
HEAD_PAD = 256
PAGE_SLOTS = 3
HEAD_GROUP = 4
SCAN_CHUNK = 64
VMEM_LIMIT = 50 * 1024 * 1024
HI = lax.Precision.HIGHEST
SCAN_EXACT = False


def _cp(*sem, vmem=VMEM_LIMIT):
    return pltpu.CompilerParams(dimension_semantics=sem, vmem_limit_bytes=vmem)


def _tile(n, pref, mult=8):
    best = None
    t = mult
    while t <= min(n, pref):
        if n % t == 0:
            best = t
        t += mult
    return best if best is not None else n


def _nt(a, b, precision=None):
    return lax.dot_general(a, b, (((1,), (1,)), ((), ())), precision=precision,
                           preferred_element_type=f32)


def _mm(a, b, precision=None):
    return jnp.dot(a, b, precision=precision, preferred_element_type=f32)


def _rms(x, g):
    return x * lax.rsqrt(jnp.mean(x * x, axis=-1, keepdims=True) + RMS_EPS) * g


def _segsum(x, ind):
    hi = x.astype(bf16)
    lo = (x - hi.astype(f32)).astype(bf16)
    return _mm(hi, ind) + _mm(lo, ind)


def _fmm_norm_kernel(x_ref, g_ref, w_ref, *rest, has_res):
    if has_res:
        r_ref, o_ref, xn_ref = rest
    else:
        o_ref, xn_ref = rest

    @pl.when(pl.program_id(1) == 0)
    def _():
        xn_ref[...] = _rms(x_ref[...], g_ref[...]).astype(bf16)

    acc = _mm(xn_ref[...], w_ref[...])
    if has_res:
        acc = acc + r_ref[...]
    o_ref[...] = acc.astype(o_ref.dtype)


def fmm_norm(x, g, w, *, tm, tn, out_dtype=f32, res=None):
    M, K = x.shape
    N = w.shape[1]
    in_specs = [pl.BlockSpec((tm, K), lambda i, j: (i, 0)),
                pl.BlockSpec((1, K), lambda i, j: (0, 0)),
                pl.BlockSpec((K, tn), lambda i, j: (0, j))]
    args = [x, g, w]
    if res is not None:
        in_specs.append(pl.BlockSpec((tm, tn), lambda i, j: (i, j)))
        args.append(res)
    return pl.pallas_call(
        functools.partial(_fmm_norm_kernel, has_res=res is not None),
        grid=(M // tm, N // tn),
        in_specs=in_specs,
        out_specs=pl.BlockSpec((tm, tn), lambda i, j: (i, j)),
        out_shape=jax.ShapeDtypeStruct((M, N), out_dtype),
        scratch_shapes=[pltpu.VMEM((tm, K), bf16)],
        compiler_params=_cp("parallel", "arbitrary"),
        name="fmm_norm",
    )(*args)


def _fmm_kernel(x_ref, w_ref, *rest, has_res, nk):
    if has_res:
        r_ref, o_ref, acc_ref = rest
    else:
        o_ref, acc_ref = rest
    k = pl.program_id(2)

    @pl.when(k == 0)
    def _():
        acc_ref[...] = jnp.zeros_like(acc_ref)

    acc_ref[...] += _mm(x_ref[...].astype(bf16), w_ref[...])

    @pl.when(k == nk - 1)
    def _():
        a = acc_ref[...]
        if has_res:
            a = a + r_ref[...]
        o_ref[...] = a.astype(o_ref.dtype)


def fmm(x, w, *, tm, tn, tk, out_dtype=f32, res=None):
    M, K = x.shape
    N = w.shape[1]
    nk = K // tk
    in_specs = [pl.BlockSpec((tm, tk), lambda i, j, k: (i, k)),
                pl.BlockSpec((tk, tn), lambda i, j, k: (k, j))]
    args = [x, w]
    if res is not None:
        in_specs.append(pl.BlockSpec((tm, tn), lambda i, j, k: (i, j)))
        args.append(res)
    return pl.pallas_call(
        functools.partial(_fmm_kernel, has_res=res is not None, nk=nk),
        grid=(M // tm, N // tn, nk),
        in_specs=in_specs,
        out_specs=pl.BlockSpec((tm, tn), lambda i, j, k: (i, j)),
        out_shape=jax.ShapeDtypeStruct((M, N), out_dtype),
        scratch_shapes=[pltpu.VMEM((tm, tn), f32)],
        compiler_params=_cp("parallel", "parallel", "arbitrary"),
        name="fmm",
    )(*args)


def _gmm_kernel(x_ref, w_ref, o_ref):
    o_ref[...] = _mm(x_ref[...].astype(bf16), w_ref[...]).astype(o_ref.dtype)


def gmm(x, w, *, tm, out_dtype=f32):
    M = x.shape[0]
    G, K, N = w.shape
    return pl.pallas_call(
        _gmm_kernel,
        grid=(M // tm, G),
        in_specs=[pl.BlockSpec((tm, K), lambda i, g: (i, g)),
                  pl.BlockSpec((None, K, N), lambda i, g: (g, 0, 0))],
        out_specs=pl.BlockSpec((tm, N), lambda i, g: (i, g)),
        out_shape=jax.ShapeDtypeStruct((M, G * N), out_dtype),
        compiler_params=_cp("parallel", "arbitrary"),
        name="gmm",
    )(x, w)


def _pool_p_kernel(u_ref, halo_ref, w_ref, sc_ref, o_ref, ext_ref, *, tm):
    i = pl.program_id(0)
    ext_ref[0:16, :] = jnp.where(i > 0, halo_ref[...], 0.0)
    ext_ref[16:, :] = u_ref[...]
    pos = i * tm + lax.broadcasted_iota(jnp.int32, (tm, 1), 0)
    for g, w in enumerate(POOL_WINDOWS):
        cols = slice(g * POOL_GW, (g + 1) * POOL_GW)
        s = ext_ref[16:, cols]
        for d in range(1, w):
            s = s + ext_ref[pl.ds(16 - d, tm), cols]
        pooled = s / jnp.minimum(pos + 1, w).astype(f32)
        diff = (pooled - ext_ref[16:, cols]).astype(bf16)
        o_ref[:, cols] = (_mm(diff, w_ref[g]) * sc_ref[:, cols]).astype(o_ref.dtype)


def pool_prompt(proj_a, pool_w, scale, *, T, tm):
    return pl.pallas_call(
        functools.partial(_pool_p_kernel, tm=tm),
        grid=(T // tm,),
        in_specs=[pl.BlockSpec((tm, MIX_W), lambda i: (i, 0)),
                  pl.BlockSpec((16, MIX_W), lambda i: (jnp.maximum(i * (tm // 16) - 1, 0), 0)),
                  pl.BlockSpec((4, POOL_GW, POOL_GW), lambda i: (0, 0, 0)),
                  pl.BlockSpec((1, MIX_W), lambda i: (0, 0))],
        out_specs=pl.BlockSpec((tm, MIX_W), lambda i: (i, 0)),
        out_shape=jax.ShapeDtypeStruct((T, MIX_W), bf16),
        scratch_shapes=[pltpu.VMEM((tm + 16, MIX_W), f32)],
        compiler_params=_cp("parallel"),
        name="pool_prompt",
    )(proj_a, proj_a, pool_w, scale)


def _pool_s_kernel(h_ref, u_ref, w_ref, sc_ref, o_ref, *, counts):
    for g, w in enumerate(POOL_WINDOWS):
        cols = slice(g * POOL_GW, (g + 1) * POOL_GW)
        u = u_ref[:, cols]
        s = u
        for d in range(1, w):
            s = s + h_ref[POOL_HIST - d, :, cols]
        diff = (s / counts[g] - u).astype(bf16)
        o_ref[:, cols] = (_mm(diff, w_ref[g]) * sc_ref[:, cols]).astype(o_ref.dtype)


def pool_sample(hist_t, proj_a, pool_w, scale, *, T, B, past_len):
    counts = tuple(float(min(past_len + 1, w)) for w in POOL_WINDOWS)
    return pl.pallas_call(
        functools.partial(_pool_s_kernel, counts=counts),
        grid=(1,),
        in_specs=[pl.BlockSpec((POOL_HIST, B, MIX_W), lambda i: (0, 0, 0)),
                  pl.BlockSpec((B, MIX_W), lambda i: (T // B, 0)),
                  pl.BlockSpec((4, POOL_GW, POOL_GW), lambda i: (0, 0, 0)),
                  pl.BlockSpec((1, MIX_W), lambda i: (0, 0))],
        out_specs=pl.BlockSpec((B, MIX_W), lambda i: (0, 0)),
        out_shape=jax.ShapeDtypeStruct((B, MIX_W), bf16),
        compiler_params=_cp("arbitrary"),
        name="pool_sample",
    )(hist_t, proj_a, pool_w, scale)


def _rope_half(hi, coef):
    rs = lax.rsqrt(jnp.sum(hi * hi, axis=-1, keepdims=True) * (0.5 / MLA_ROPE) + RMS_EPS)
    t = hi * rs * coef
    r = t + pltpu.roll(t, 64, 1)
    lane = lax.broadcasted_iota(jnp.int32, (1, 128), 1)
    return jnp.where(lane < MLA_ROPE, r, 0.0)


def _qprep_kernel(c_ref, g_ref, w_ref, gq_ref, tab_ref, o_ref):
    cn = _rms(c_ref[...], g_ref[...]).astype(bf16)
    x = _mm(cn, w_ref[...])
    gq = gq_ref[...]
    coef = tab_ref[...] * gq[:, MLA_NOPE:]
    for h in range(MLA_HEADS):
        b = h * HEAD_PAD
        o_ref[:, b:b + MLA_NOPE] = _rms(x[:, b:b + MLA_NOPE], gq[:, :MLA_NOPE]).astype(o_ref.dtype)
        o_ref[:, b + MLA_NOPE:b + HEAD_PAD] = _rope_half(x[:, b + MLA_NOPE:b + HEAD_PAD], coef).astype(o_ref.dtype)


def qprep(proj_a, cq_g, w_big, gq, tab, *, tm):
    M = proj_a.shape[0]
    return pl.pallas_call(
        _qprep_kernel,
        grid=(M // tm,),
        in_specs=[pl.BlockSpec((tm, MLA_Q_LORA), lambda i: (i, OFF_CQ // MLA_Q_LORA)),
                  pl.BlockSpec((1, MLA_Q_LORA), lambda i: (0, 0)),
                  pl.BlockSpec((MLA_Q_LORA, MLA_HEADS * HEAD_PAD), lambda i: (0, 0)),
                  pl.BlockSpec((1, HEAD_PAD), lambda i: (0, 0)),
                  pl.BlockSpec((tm, 128), lambda i: (i, 0))],
        out_specs=pl.BlockSpec((tm, MLA_HEADS * HEAD_PAD), lambda i: (i, 0)),
        out_shape=jax.ShapeDtypeStruct((M, MLA_HEADS * HEAD_PAD), bf16),
        compiler_params=_cp("parallel"),
        name="qprep",
    )(proj_a, cq_g, w_big, gq, tab)


def _kprep_kernel(c_ref, g_ref, gk_ref, tab_ref, w_ref, kng_ref, lat_ref, latb_ref, kr_ref, k_ref):
    ck = c_ref[...]
    lat = _rms(ck[:, :MLA_KV_LORA], g_ref[...])
    lat_ref[...] = lat
    latb = lat.astype(bf16)
    latb_ref[...] = latb
    kr = _rope_half(ck[:, MLA_KV_LORA:], tab_ref[...] * gk_ref[...])
    kr_ref[...] = kr
    krb = kr.astype(bf16)
    kn = _mm(latb, w_ref[...])
    for h in range(MLA_HEADS):
        b = h * HEAD_PAD
        k_ref[:, b:b + MLA_NOPE] = _rms(kn[:, h * MLA_NOPE:(h + 1) * MLA_NOPE], kng_ref[...]).astype(bf16)
        k_ref[:, b + MLA_NOPE:b + HEAD_PAD] = krb


def kprep(proj_a, ckv_g, gk, tab, w_uk, kn_g, *, tm):
    M = proj_a.shape[0]
    wk = MLA_KV_LORA + 128
    row = lambda n: pl.BlockSpec((tm, n), lambda i: (i, 0))
    return pl.pallas_call(
        _kprep_kernel,
        grid=(M // tm,),
        in_specs=[pl.BlockSpec((tm, wk), lambda i: (i, OFF_CKV // wk)),
                  pl.BlockSpec((1, MLA_KV_LORA), lambda i: (0, 0)),
                  pl.BlockSpec((1, 128), lambda i: (0, 0)),
                  row(128),
                  pl.BlockSpec((MLA_KV_LORA, MLA_HEADS * MLA_NOPE), lambda i: (0, 0)),
                  pl.BlockSpec((1, MLA_NOPE), lambda i: (0, 0))],
        out_specs=[row(MLA_KV_LORA), row(MLA_KV_LORA), row(128), row(MLA_HEADS * HEAD_PAD)],
        out_shape=[jax.ShapeDtypeStruct((M, MLA_KV_LORA), f32),
                   jax.ShapeDtypeStruct((M, MLA_KV_LORA), bf16),
                   jax.ShapeDtypeStruct((M, 128), f32),
                   jax.ShapeDtypeStruct((M, MLA_HEADS * HEAD_PAD), bf16)],
        compiler_params=_cp("parallel"),
        name="kprep",
    )(proj_a, ckv_g, gk, tab, w_uk, kn_g)


def _attn_p_kernel(q_ref, k_ref, v_ref, o_ref, m_ref, l_ref, acc_ref, *, tq):
    qi = pl.program_id(0)
    ki = pl.program_id(1)

    @pl.when(ki == 0)
    def _():
        m_ref[...] = jnp.full_like(m_ref, NEG_INF)
        l_ref[...] = jnp.zeros_like(l_ref)
        acc_ref[...] = jnp.zeros_like(acc_ref)

    def step(masked):
        if masked:
            rows = lax.broadcasted_iota(jnp.int32, (tq, tq), 0)
            cols = lax.broadcasted_iota(jnp.int32, (tq, tq), 1)
            bias = jnp.where(cols <= rows, 0.0, NEG_INF)
        v = v_ref[...]
        rep = tq // 128
        for h0 in range(0, MLA_HEADS, HEAD_GROUP):
            H = range(h0, h0 + HEAD_GROUP)
            hs = {h: slice(h * HEAD_PAD, (h + 1) * HEAD_PAD) for h in H}
            s = {h: _nt(q_ref[:, hs[h]], k_ref[:, hs[h]]) for h in H}
            if masked:
                s = {h: s[h] + bias for h in H}
            m_prev = {h: m_ref[h] for h in H}
            m_new = {h: jnp.maximum(m_prev[h], jnp.max(s[h], axis=1, keepdims=True)) for h in H}
            p = {h: jnp.exp(s[h] - jnp.concatenate([m_new[h]] * rep, axis=1)) for h in H}
            alpha = {h: jnp.exp(m_prev[h] - m_new[h]) for h in H}
            pv = {h: _mm(p[h].astype(bf16), v) for h in H}
            for h in H:
                l_ref[h] = alpha[h] * l_ref[h] + jnp.sum(p[h], axis=1, keepdims=True)
                acc_ref[h] = jnp.concatenate([alpha[h]] * (MLA_KV_LORA // 128), axis=1) * acc_ref[h] + pv[h]
                m_ref[h] = m_new[h]

    @pl.when(ki < qi)
    def _():
        step(False)

    @pl.when(ki == qi)
    def _():
        step(True)

    @pl.when(ki == qi)
    def _():
        for h in range(MLA_HEADS):
            hs = slice(h * HEAD_PAD, (h + 1) * HEAD_PAD)
            o_ref[:, hs] = (acc_ref[h] / l_ref[h][:, :1]).astype(o_ref.dtype)


def attn_prompt(q_full, k_full, lat_b, *, T, tq):
    n = T // tq
    W = MLA_HEADS * HEAD_PAD
    return pl.pallas_call(
        functools.partial(_attn_p_kernel, tq=tq),
        grid=(n, n),
        in_specs=[pl.BlockSpec((tq, W), lambda qi, ki: (qi, 0)),
                  pl.BlockSpec((tq, W), lambda qi, ki: (jnp.minimum(ki, qi), 0)),
                  pl.BlockSpec((tq, MLA_KV_LORA), lambda qi, ki: (jnp.minimum(ki, qi), 0))],
        out_specs=pl.BlockSpec((tq, W), lambda qi, ki: (qi, 0)),
        out_shape=jax.ShapeDtypeStruct((T, W), bf16),
        scratch_shapes=[pltpu.VMEM((MLA_HEADS, tq, 128), f32),
                        pltpu.VMEM((MLA_HEADS, tq, 128), f32),
                        pltpu.VMEM((MLA_HEADS, tq, MLA_KV_LORA), f32)],
        compiler_params=_cp("parallel", "arbitrary"),
        name="attn_prompt",
    )(q_full, k_full, lat_b)


def _attn_s_kernel(pt_ref, qa_ref, qr_ref, qf_ref, kn_ref, ln_ref, wt_ref, lat_hbm, rope_hbm, o_ref,
                   lat_buf, rope_buf, sem, m_ref, l_ref, acc_ref, *, G, ng, layer, nsteps):
    g = pl.program_id(1)
    s = pl.program_id(0) * ng + g
    ahead = PAGE_SLOTS - 1

    def page_copies(step, slot):
        out = []
        for i in range(G):
            page = pt_ref[step * G + i]
            out.append(pltpu.make_async_copy(lat_hbm.at[layer, page], lat_buf.at[slot, i], sem.at[slot]))
            out.append(pltpu.make_async_copy(rope_hbm.at[layer, page], rope_buf.at[slot, i], sem.at[slot]))
        return out

    @pl.when(s == 0)
    def _():
        for k in range(ahead):
            for c in page_copies(jnp.minimum(k, nsteps - 1), k):
                c.start()

    @pl.when(g == 0)
    def _():
        m_ref[...] = jnp.full_like(m_ref, NEG_INF)
        l_ref[...] = jnp.zeros_like(l_ref)
        acc_ref[...] = jnp.zeros_like(acc_ref)

    slot = lax.rem(s, PAGE_SLOTS)
    for c in page_copies(s, slot):
        c.wait()

    qa = qa_ref[...]
    qr = qr_ref[...]
    wt = wt_ref[...]
    lat_all = jnp.concatenate([lat_buf[slot, i].astype(bf16) for i in range(G)], axis=0)
    pairs = range(0, G, 2)
    lat = [lat_all[i * PAGE_SIZE:(i + 2) * PAGE_SIZE] for i in pairs]
    rope_t = [jnp.concatenate([rope_buf[slot, i], rope_buf[slot, i + 1]], axis=1).astype(bf16) for i in pairs]
    kt = [_nt(wt, x) for x in lat]
    num = [_nt(qa, x) for x in lat]
    rr = [_mm(qr, x) for x in rope_t]
    ssq = [jnp.sum((x * x).reshape(MLA_HEADS, MLA_NOPE, 2 * PAGE_SIZE), axis=1) for x in kt]
    parts = [n * lax.rsqrt(q * (1.0 / MLA_NOPE) + RMS_EPS) + r for n, q, r in zip(num, ssq, rr)]
    sc = jnp.concatenate(parts, axis=1)
    m_prev = m_ref[...]
    m_new = jnp.maximum(m_prev, jnp.max(sc, axis=1, keepdims=True))
    alpha = jnp.exp(m_prev - m_new)
    p = jnp.exp(sc - m_new[:, :1])
    l_ref[...] = alpha * l_ref[...] + jnp.sum(p, axis=1, keepdims=True)
    acc_ref[...] = alpha[:, :1] * acc_ref[...] + _mm(p.astype(bf16), lat_all)
    m_ref[...] = m_new

    for c in page_copies(jnp.minimum(s + ahead, nsteps - 1), lax.rem(s + ahead, PAGE_SLOTS)):
        c.start()

    @pl.when(g == ng - 1)
    def _():
        s_new = jnp.sum(qf_ref[...].astype(f32) * kn_ref[...].astype(f32), axis=1, keepdims=True)
        m_prev = m_ref[...]
        m_new = jnp.maximum(m_prev, s_new)
        alpha = jnp.exp(m_prev - m_new)
        p = jnp.exp(s_new - m_new)
        l = alpha * l_ref[...] + p
        acc = alpha[:, :1] * acc_ref[...] + p[:, :1] * ln_ref[...].astype(bf16).astype(f32)
        o_ref[...] = (acc / l[:, :1]).astype(o_ref.dtype)

    @pl.when(s == nsteps - 1)
    def _():
        for k in range(1, ahead + 1):
            for c in page_copies(nsteps - 1, lax.rem(s + k, PAGE_SLOTS)):
                c.wait()


def attn_sample(page_table, qa, qr, qf, knew, latnew, wuk_t, cache_lat, cache_rope_t, *, layer, G):
    B, n_pages = page_table.shape
    ng = n_pages // G
    pt = page_table.reshape(-1)
    per_b = lambda *shape: pl.BlockSpec((None,) + shape, lambda b, g, pt: (b,) + (0,) * len(shape))
    in_specs = [per_b(MLA_HEADS, MLA_KV_LORA), per_b(MLA_HEADS, MLA_ROPE), per_b(MLA_HEADS, HEAD_PAD),
                per_b(MLA_HEADS, HEAD_PAD), per_b(1, MLA_KV_LORA),
                pl.BlockSpec((MLA_HEADS * MLA_NOPE, MLA_KV_LORA), lambda b, g, pt: (0, 0)),
                pl.BlockSpec(memory_space=pl.ANY), pl.BlockSpec(memory_space=pl.ANY)]
    return pl.pallas_call(
        functools.partial(_attn_s_kernel, G=G, ng=ng, layer=layer, nsteps=B * ng),
        grid_spec=pltpu.PrefetchScalarGridSpec(
            num_scalar_prefetch=1,
            grid=(B, ng),
            in_specs=in_specs,
            out_specs=per_b(MLA_HEADS, MLA_KV_LORA),
            scratch_shapes=[pltpu.VMEM((PAGE_SLOTS, G, PAGE_SIZE, MLA_KV_LORA), f32),
                            pltpu.VMEM((PAGE_SLOTS, G, MLA_ROPE, PAGE_SIZE), f32),
                            pltpu.SemaphoreType.DMA((PAGE_SLOTS,)),
                            pltpu.VMEM((MLA_HEADS, 128), f32),
                            pltpu.VMEM((MLA_HEADS, 128), f32),
                            pltpu.VMEM((MLA_HEADS, MLA_KV_LORA), f32)]),
        out_shape=jax.ShapeDtypeStruct((B, MLA_HEADS, MLA_KV_LORA), bf16),
        compiler_params=_cp("arbitrary", "arbitrary"),
        name="attn_sample",
    )(pt, qa, qr, qf, knew, latnew, wuk_t, cache_lat, cache_rope_t)


def _rwkv_prep_math(z, zs, mu_ref, wcat_ref, w0_ref, a0_ref, kk_ref, ka_ref, rk_ref, ind_ref, outs):
    r_o, k_o, v_o, kk_o, b_o, lw_o, g_o, bonus_o = outs
    zm = z + (zs - z) * mu_ref[...]
    r = zm[:, :MIX_W]
    k = zm[:, MIX_W:2 * MIX_W]
    v = zm[:, 2 * MIX_W:3 * MIX_W]
    tail = zm[:, 3 * MIX_W:]
    lane = lax.broadcasted_iota(jnp.int32, (1, 256), 1)
    act = jnp.where(lane < 64, jnp.tanh(tail), jnp.where(lane < 128, tail, 1.0 / (1.0 + jnp.exp(-tail))))
    lo = _mm(act.astype(bf16), wcat_ref[...])
    u = -(w0_ref[...] + lo[:, :MIX_W])
    w = -(jnp.maximum(u, 0.0) + jnp.log(1.0 + jnp.exp(-jnp.abs(u)))) - 0.5
    a = 1.0 / (1.0 + jnp.exp(-(a0_ref[...] + lo[:, MIX_W:2 * MIX_W])))
    ind = ind_ref[...]
    kk = k * kk_ref[...]
    kk = kk * lax.rsqrt(_segsum(kk * kk, ind) + 1e-12)
    k_eff = k * (1.0 + (a - 1.0) * ka_ref[...])
    r_o[...] = r
    k_o[...] = k_eff
    v_o[...] = v
    kk_o[...] = kk
    b_o[...] = kk * a
    lw_o[...] = -jnp.exp(w)
    g_o[...] = lo[:, 2 * MIX_W:]
    bonus_o[...] = _segsum(r * k_eff * rk_ref[...], ind) * v


def _rwkv_prep_p_kernel(z_ref, halo_ref, *rest):
    params, outs = rest[:8], rest[8:]
    z = z_ref[...]
    first = pl.program_id(0) == 0
    prev = jnp.where(first, 0.0, halo_ref[7:8, :])
    row = lax.broadcasted_iota(jnp.int32, (z.shape[0], 1), 0)
    zs = jnp.where(row == 0, prev, pltpu.roll(z, 1, 0))
    _rwkv_prep_math(z, zs, *params, outs)


def _rwkv_prep_s_kernel(z_ref, zs_ref, *rest):
    params, outs = rest[:8], rest[8:]
    _rwkv_prep_math(z_ref[...], zs_ref[...], *params, outs)


def rwkv_prep(proj_z, shift, params, *, T, B, tm):
    const = lambda a: pl.BlockSpec(a.shape, lambda i: (0,) * a.ndim)
    pspecs = [const(p) for p in params]
    outs_p = [jax.ShapeDtypeStruct((T, MIX_W), f32)] * 8
    res_p = pl.pallas_call(
        _rwkv_prep_p_kernel,
        grid=(T // tm,),
        in_specs=[pl.BlockSpec((tm, RWKV_PROJ), lambda i: (i, 0)),
                  pl.BlockSpec((8, RWKV_PROJ), lambda i: (jnp.maximum(i * (tm // 8) - 1, 0), 0))] + pspecs,
        out_specs=[pl.BlockSpec((tm, MIX_W), lambda i: (i, 0))] * 8,
        out_shape=outs_p,
        compiler_params=_cp("parallel"),
        name="rwkv_prep_prompt",
    )(proj_z, proj_z, *params)
    res_s = pl.pallas_call(
        _rwkv_prep_s_kernel,
        grid=(1,),
        in_specs=[pl.BlockSpec((B, RWKV_PROJ), lambda i: (T // B, 0)),
                  pl.BlockSpec((B, RWKV_PROJ), lambda i: (0, 0))] + pspecs,
        out_specs=[pl.BlockSpec((B, MIX_W), lambda i: (0, 0))] * 8,
        out_shape=[jax.ShapeDtypeStruct((B, MIX_W), f32)] * 8,
        compiler_params=_cp("arbitrary"),
        name="rwkv_prep_sample",
    )(proj_z, shift, *params)
    return res_p, res_s


def _scan_kernel(r_ref, k_ref, v_ref, kk_ref, b_ref, lw_ref, lwt_ref, bt_ref, kt_ref,
                 y_ref, so_ref, st_ref, *, exact):
    C = SCAN_CHUNK
    N = RWKV_HEAD

    @pl.when(pl.program_id(0) == 0)
    def _():
        st_ref[...] = jnp.zeros_like(st_ref)

    row = lax.broadcasted_iota(jnp.int32, (C, C), 0)
    col = lax.broadcasted_iota(jnp.int32, (C, C), 1)
    strict = col < row
    incl = col <= row
    eye = (row == col).astype(f32)
    if exact:
        op = lambda x: x
        mm = functools.partial(_mm, precision=HI)
        nt = functools.partial(_nt, precision=HI)
    else:
        op = lambda x: x.astype(bf16)
        mm, nt = _mm, _nt

    lw = lw_ref[...]
    cs = _mm(incl.astype(f32), lw, precision=HI)
    cst = _mm(lwt_ref[...], (row <= col).astype(f32), precision=HI)
    em = jnp.exp(-cs)
    emt = jnp.exp(-cst)
    ar_all = op(jnp.concatenate([-kk_ref[...] * jnp.exp(cs - lw), r_ref[...] * jnp.exp(cs)], axis=0))
    bh_all = op(b_ref[...] * em)
    kh_all = op(k_ref[...] * em)
    bkt_all = op(jnp.concatenate([bt_ref[...] * emt, kt_ref[...] * emt], axis=1))
    g_all = jnp.exp(cst[:, C - 1:C])
    v_all = op(v_ref[...])
    H = range(RWKV_HEADS)
    hsl = [slice(h * N, (h + 1) * N) for h in H]
    ar = [ar_all[:, s] for s in hsl]
    v = [v_all[:, s] for s in hsl]
    pb = [nt(ar[h], bh_all[:, hsl[h]]) for h in H]
    pk = [nt(ar[h], kh_all[:, hsl[h]]) for h in H]
    pw = [jnp.where(strict, pb[h][:C], 0.0) for h in H]
    tinv = [eye + pw[h] for h in H]
    for _ in range(5):
        pw = [mm(op(pw[h]), op(pw[h])) for h in H]
        tinv = [tinv[h] + mm(op(tinv[h]), op(pw[h])) for h in H]
    st = [st_ref[h] for h in H]
    xs = [mm(ar[h], op(st[h])) for h in H]
    pv = [mm(op(jnp.concatenate([jnp.where(strict, pk[h][:C], 0.0), jnp.where(incl, pk[h][C:], 0.0)], axis=0)), v[h])
          for h in H]
    ut = [mm(op(tinv[h]), op(xs[h][:C] + pv[h][:C])) for h in H]
    for h in H:
        y_ref[:, hsl[h]] = xs[h][C:] + pv[h][C:] + mm(op(jnp.where(incl, pb[h][C:], 0.0)), op(ut[h]))
    for h in H:
        uv = jnp.concatenate([op(ut[h]), v[h]], axis=0)
        st_ref[h] = g_all[hsl[h]] * (st[h] + mm(bkt_all[hsl[h]], uv))
    so_ref[...] = st_ref[...]


def rwkv_scan(r, k, v, kk, b, lw, *, exact):
    T = r.shape[0]
    C = SCAN_CHUNK
    nc = T // C
    tr = lambda x: x.reshape(nc, C, MIX_W).transpose(0, 2, 1)
    row = pl.BlockSpec((C, MIX_W), lambda c: (c, 0))
    colm = pl.BlockSpec((None, MIX_W, C), lambda c: (c, 0, 0))
    return pl.pallas_call(
        functools.partial(_scan_kernel, exact=exact),
        grid=(nc,),
        in_specs=[row] * 6 + [colm] * 3,
        out_specs=[row, pl.BlockSpec((RWKV_HEADS, RWKV_HEAD, RWKV_HEAD), lambda c: (0, 0, 0))],
        out_shape=[jax.ShapeDtypeStruct((T, MIX_W), f32),
                   jax.ShapeDtypeStruct((RWKV_HEADS, RWKV_HEAD, RWKV_HEAD), f32)],
        scratch_shapes=[pltpu.VMEM((RWKV_HEADS, RWKV_HEAD, RWKV_HEAD), f32)],
        compiler_params=_cp("arbitrary"),
        name="rwkv_scan",
    )(r, k, v, kk, b, lw, tr(lw), tr(b), tr(k))


def _rwkv_step_kernel(s_ref, r_ref, k_ref, v_ref, kk_ref, b_ref, lw_ref, so_ref, y_ref):
    s = s_ref[...]
    sa = jnp.sum(s * (-kk_ref[...])[None], axis=1, keepdims=True)
    s_new = s * jnp.exp(lw_ref[...])[None] + sa * b_ref[...][None] + v_ref[...] * k_ref[...][None]
    so_ref[...] = s_new
    y_ref[...] = jnp.sum(s_new * r_ref[...][None], axis=1, keepdims=True)


def rwkv_step(state, r, k, v, kk, b, lw):
    B = state.shape[0]
    H, N = RWKV_HEADS, RWKV_HEAD
    st = state.transpose(1, 2, 3, 0)
    keyv = lambda x: x.T.reshape(H, N, B)
    kspec = pl.BlockSpec((None, N, B), lambda h: (h, 0, 0))
    cspec = pl.BlockSpec((None, N, 1, B), lambda h: (h, 0, 0, 0))
    sspec = pl.BlockSpec((None, N, N, B), lambda h: (h, 0, 0, 0))
    s_new, y = pl.pallas_call(
        _rwkv_step_kernel,
        grid=(H,),
        in_specs=[sspec, kspec, kspec, cspec, kspec, kspec, kspec],
        out_specs=[sspec, cspec],
        out_shape=[jax.ShapeDtypeStruct((H, N, N, B), f32), jax.ShapeDtypeStruct((H, N, 1, B), f32)],
        compiler_params=_cp("parallel"),
        name="rwkv_step",
    )(st, keyv(r), keyv(k), keyv(v).reshape(H, N, 1, B), keyv(kk), keyv(b), keyv(lw))
    return s_new.transpose(3, 0, 1, 2), y.reshape(H * N, B).T


def _rwkv_post_kernel(y_ref, bonus_ref, g_ref, lnw_ref, lnb_ref, ind_ref, o_ref):
    ind = ind_ref[...]
    y = y_ref[...]
    d = y - _segsum(y, ind) * (1.0 / RWKV_HEAD)
    var = _segsum(d * d, ind) * (1.0 / RWKV_HEAD)
    yn = d * lax.rsqrt(var + GN_EPS) * lnw_ref[...] + lnb_ref[...]
    o_ref[...] = ((yn + bonus_ref[...]) * g_ref[...]).astype(o_ref.dtype)


def rwkv_post(y, bonus, g, ln_w, ln_b, ind, *, tm):
    M = y.shape[0]
    row = pl.BlockSpec((tm, MIX_W), lambda i: (i, 0))
    vec = pl.BlockSpec((1, MIX_W), lambda i: (0, 0))
    return pl.pallas_call(
        _rwkv_post_kernel,
        grid=(M // tm,),
        in_specs=[row, row, row, vec, vec, pl.BlockSpec((MIX_W, MIX_W), lambda i: (0, 0))],
        out_specs=row,
        out_shape=jax.ShapeDtypeStruct((M, MIX_W), bf16),
        compiler_params=_cp("parallel"),
        name="rwkv_post",
    )(y, bonus, g, ln_w, ln_b, ind)


def _merge_kernel(b0_ref, b1_ref, b2_ref, w_ref, g0_ref, g1_ref, g2_ref, o_ref):
    acc = None
    for n, (b_ref, g_ref) in enumerate(((b0_ref, g0_ref), (b1_ref, g1_ref), (b2_ref, g2_ref))):
        gate = 1.0 / (1.0 + jnp.exp(-g_ref[...]))
        t = gate * _mm(b_ref[...], w_ref[n])
        acc = t if acc is None else acc + t
    o_ref[...] = acc.astype(o_ref.dtype)


def merge(branches, w_branch, proj_g, *, tm, tn):
    M = proj_g.shape[0]
    nb = D_MODEL // tn
    bspec = pl.BlockSpec((tm, MIX_W), lambda i, j: (i, 0))
    gspec = lambda n: pl.BlockSpec((tm, tn), lambda i, j: (i, n * nb + j))
    return pl.pallas_call(
        _merge_kernel,
        grid=(M // tm, nb),
        in_specs=[bspec, bspec, bspec, pl.BlockSpec((3, MIX_W, tn), lambda i, j: (0, 0, j)),
                  gspec(0), gspec(1), gspec(2)],
        out_specs=pl.BlockSpec((tm, tn), lambda i, j: (i, j)),
        out_shape=jax.ShapeDtypeStruct((M, D_MODEL), bf16),
        compiler_params=_cp("parallel", "arbitrary"),
        name="merge",
    )(*branches, w_branch, proj_g, proj_g, proj_g)


def _memkv_kernel(kv_ref, g_ref, k_ref, v_ref):
    for h in range(MEM_HEADS):
        hs = slice(h * MEM_HEAD_DIM, (h + 1) * MEM_HEAD_DIM)
        k_ref[:, hs] = _rms(kv_ref[:, hs], g_ref[...])
    v_ref[...] = kv_ref[:, MEM_W:]


def memkv_post(kv, g):
    n = kv.shape[0]
    return pl.pallas_call(
        _memkv_kernel,
        grid=(1,),
        in_specs=[pl.BlockSpec((n, 2 * MEM_W), lambda i: (0, 0)), pl.BlockSpec((1, MEM_HEAD_DIM), lambda i: (0, 0))],
        out_specs=[pl.BlockSpec((n, MEM_W), lambda i: (0, 0))] * 2,
        out_shape=[jax.ShapeDtypeStruct((n, MEM_W), f32)] * 2,
        compiler_params=_cp("arbitrary"),
        name="memkv_post",
    )(kv, g)


def _memattn_p_kernel(q_ref, g_ref, k_ref, v_ref, o_ref):
    for h in range(MEM_HEADS):
        hs = slice(h * MEM_HEAD_DIM, (h + 1) * MEM_HEAD_DIM)
        q = _rms(q_ref[:, hs], g_ref[...]).astype(bf16)
        s = _nt(q, k_ref[:, hs].astype(bf16)) * (MEM_HEAD_DIM ** -0.5)
        p = jnp.exp(s - jnp.max(s, axis=1, keepdims=True))
        o = _mm(p.astype(bf16), v_ref[:, hs].astype(bf16)) / jnp.sum(p, axis=1, keepdims=True)
        o_ref[:, hs] = o.astype(o_ref.dtype)


def memattn_prompt(q, g, mem_k, mem_v, *, T, tm):
    n = mem_k.shape[0]
    return pl.pallas_call(
        _memattn_p_kernel,
        grid=(T // tm,),
        in_specs=[pl.BlockSpec((tm, MEM_W), lambda i: (i, 0)), pl.BlockSpec((1, MEM_HEAD_DIM), lambda i: (0, 0)),
                  pl.BlockSpec((n, MEM_W), lambda i: (0, 0)), pl.BlockSpec((n, MEM_W), lambda i: (0, 0))],
        out_specs=pl.BlockSpec((tm, MEM_W), lambda i: (i, 0)),
        out_shape=jax.ShapeDtypeStruct((T, MEM_W), bf16),
        compiler_params=_cp("parallel"),
        name="memattn_prompt",
    )(q, g, mem_k, mem_v)


def _memattn_s_kernel(q_ref, g_ref, k_ref, v_ref, o_ref):
    for h in range(MEM_HEADS):
        hs = slice(h * MEM_HEAD_DIM, (h + 1) * MEM_HEAD_DIM)
        q = _rms(q_ref[:, :, hs], g_ref[...])
        s = jnp.sum(k_ref[:, :, h, :] * q, axis=-1, keepdims=True) * (MEM_HEAD_DIM ** -0.5)
        p = jnp.exp(s - jnp.max(s, axis=1, keepdims=True))
        o = jnp.sum(p * v_ref[:, :, h, :], axis=1, keepdims=True) / jnp.sum(p, axis=1, keepdims=True)
        o_ref[:, :, hs] = o.astype(o_ref.dtype)


def memattn_sample(q, g, cache_k, cache_v, *, layer, T, B, tb):
    n = cache_k.shape[2]
    q3 = q.reshape(q.shape[0], 1, MEM_W)
    ck, cv = cache_k, cache_v
    kvspec = pl.BlockSpec((None, tb, n, MEM_HEADS, MEM_HEAD_DIM), lambda i: (layer, i, 0, 0, 0))
    out = pl.pallas_call(
        _memattn_s_kernel,
        grid=(B // tb,),
        in_specs=[pl.BlockSpec((tb, 1, MEM_W), lambda i: (T // tb + i, 0, 0)),
                  pl.BlockSpec((1, MEM_HEAD_DIM), lambda i: (0, 0)), kvspec, kvspec],
        out_specs=pl.BlockSpec((tb, 1, MEM_W), lambda i: (i, 0, 0)),
        out_shape=jax.ShapeDtypeStruct((B, 1, MEM_W), bf16),
        compiler_params=_cp("parallel"),
        name="memattn_sample",
    )(q3, g, ck, cv)
    return out.reshape(B, MEM_W)


def _ffn1_kernel(x_ref, g_ref, wg_ref, wu_ref, *rest, scaled):
    if scaled:
        c_ref, o_ref, xn_ref = rest
    else:
        o_ref, xn_ref = rest

    @pl.when(pl.program_id(1) == 0)
    def _():
        xn_ref[...] = _rms(x_ref[...], g_ref[...]).astype(bf16)

    xn = xn_ref[...]
    a = _mm(xn, wg_ref[...])
    a = a / (1.0 + jnp.exp(-a)) * _mm(xn, wu_ref[...])
    if scaled:
        a = a * c_ref[:, :1]
    o_ref[...] = a.astype(o_ref.dtype)


def ffn1(x, g, wg, wu, *, tm, tn, comb=None):
    M, K = x.shape
    F = wg.shape[0] * wg.shape[2]
    nper = wg.shape[2] // tn
    wspec = pl.BlockSpec((None, K, tn), lambda i, j: (j // nper, 0, j % nper))
    in_specs = [pl.BlockSpec((tm, K), lambda i, j: (i, 0)), pl.BlockSpec((1, K), lambda i, j: (0, 0)), wspec, wspec]
    args = [x, g, wg, wu]
    if comb is not None:
        in_specs.append(pl.BlockSpec((tm, 128), lambda i, j: (i, j // nper)))
        args.append(comb)
    return pl.pallas_call(
        functools.partial(_ffn1_kernel, scaled=comb is not None),
        grid=(M // tm, F // tn),
        in_specs=in_specs,
        out_specs=pl.BlockSpec((tm, tn), lambda i, j: (i, j)),
        out_shape=jax.ShapeDtypeStruct((M, F), bf16),
        scratch_shapes=[pltpu.VMEM((tm, K), bf16)],
        compiler_params=_cp("parallel", "arbitrary"),
        name="ffn1",
    )(*args)


def _router_kernel(x_ref, g_ref, w_ref, b_ref, e_ref, o_ref):
    logits = _mm(_rms(x_ref[...], g_ref[...]), w_ref[...], precision=HI) + b_ref[...]
    lane = lax.broadcasted_iota(jnp.int32, logits.shape, 1)
    logits = jnp.where(lane < N_EXPERTS, logits, -jnp.inf)
    m1 = jnp.max(logits, axis=1, keepdims=True)
    i1 = jnp.min(jnp.where(logits == m1, lane, 128), axis=1, keepdims=True)
    rest = jnp.where(lane == i1, -jnp.inf, logits)
    m2 = jnp.max(rest, axis=1, keepdims=True)
    i2 = jnp.min(jnp.where(rest == m2, lane, 128), axis=1, keepdims=True)
    e2 = jnp.exp(m2 - m1)
    w1 = 1.0 / (1.0 + e2)
    w2 = e2 / (1.0 + e2)
    comb = jnp.where(lane == i1, w1, 0.0) + jnp.where(lane == i2, w2, 0.0)
    o_ref[...] = _mm(comb, e_ref[...], precision=HI)


def router(x, g, w_pad, b_pad, expand, *, tm):
    M, K = x.shape
    return pl.pallas_call(
        _router_kernel,
        grid=(M // tm,),
        in_specs=[pl.BlockSpec((tm, K), lambda i: (i, 0)), pl.BlockSpec((1, K), lambda i: (0, 0)),
                  pl.BlockSpec((K, 128), lambda i: (0, 0)), pl.BlockSpec((1, 128), lambda i: (0, 0)),
                  pl.BlockSpec((128, N_EXPERTS * 128), lambda i: (0, 0))],
        out_specs=pl.BlockSpec((tm, N_EXPERTS * 128), lambda i: (i, 0)),
        out_shape=jax.ShapeDtypeStruct((M, N_EXPERTS * 128), f32),
        compiler_params=_cp("parallel"),
        name="router",
    )(x, g, w_pad, b_pad, expand)


def _swap_halves(x):
    h = x.shape[-1] // 2
    return jnp.concatenate([x[..., h:], x[..., :h]], axis=-1)


def kernel(x_prompt, x_sample, mem_prompt, state_pool, cache_mla_latent, cache_mla_rope, state_rwkv, state_rwkv_shift, cache_mem_k, cache_mem_v, page_table, norm_mix_g, w_in, pool_w, pool_scale, mla_cq_g, mla_w_uq, mla_ckv_g, mla_kr_g, mla_w_uk, mla_w_uv, mla_qn_g, mla_qr_g, mla_kn_g, rwkv_mu, rwkv_w0, rwkv_w2, rwkv_a0, rwkv_a2, rwkv_g2, rwkv_k_k, rwkv_k_a, rwkv_r_k, rwkv_ln_w, rwkv_ln_b, w_branch, w_out, norm_mem_g, mem_norm_g, w_q_mem, w_k_mem, w_v_mem, mem_qn_g, mem_kn_g, w_o_mem, norm_ffn_g, ffn_w_gate, ffn_w_up, ffn_w_down, moe_router, moe_router_b, moe_w_gate, moe_w_up, moe_w_down):
    depth = w_in.shape[0]
    T = x_prompt.shape[1]
    B = x_sample.shape[0]
    assert x_prompt.shape[0] == 1 and x_sample.shape[1] == 1
    n_pages = page_table.shape[1]
    past_len = n_pages * PAGE_SIZE
    M = T + B
    assert T % B == 0 and T % SCAN_CHUNK == 0 and B % 8 == 0

    tm = _tile(M, 640)
    tp = _tile(T, 512)
    tq = _tile(T, 512, 128)
    tb = _tile(B, 8)
    G = _tile(n_pages, 16, 2)
    row = lambda v: v.reshape(1, -1)

    x = jnp.concatenate([x_prompt[0], x_sample[:, 0]], axis=0)
    cache_rope_t = jnp.swapaxes(cache_mla_rope, 2, 3)

    pos = jnp.concatenate([jnp.arange(T, dtype=jnp.int32), jnp.full((B,), past_len, jnp.int32)])
    inv = ROPE_THETA ** (-jnp.arange(0, MLA_ROPE, 2, dtype=f32) / MLA_ROPE)
    ang = pos.astype(f32)[:, None] * inv[None, :]
    cos, sin = jnp.cos(ang), jnp.sin(ang)
    tab = jnp.concatenate([cos, cos, -sin, sin], axis=1)

    ids = jnp.arange(MIX_W) // RWKV_HEAD
    ind = (ids[:, None] == ids[None, :]).astype(bf16)
    expand = (jnp.arange(128)[:, None] == (jnp.arange(N_EXPERTS * 128) // 128)[None, :]).astype(f32)

    outs = {k: [] for k in ("pool_p", "lat_p", "rope_p", "rwkv_p", "shift_p", "memk_p", "memv_p",
                            "pool_s", "lat_s", "rope_s", "rwkv_s", "shift_s")}
    for l in range(depth):
        wi = w_in[l]
        kr_cols = wi[:, OFF_KR:OFF_RWKV]
        w_a = jnp.concatenate([wi[:, :OFF_KR], kr_cols, _swap_halves(kr_cols)], axis=1).astype(bf16)
        w_z = wi[:, OFF_RWKV:OFF_GATE].astype(bf16)
        w_g = wi[:, OFF_GATE:].astype(bf16)
        uq = mla_w_uq[l].reshape(MLA_Q_LORA, MLA_HEADS, MLA_QK)
        w_big = jnp.concatenate([uq, _swap_halves(uq[..., MLA_NOPE:])], axis=-1)
        w_big = w_big.reshape(MLA_Q_LORA, MLA_HEADS * HEAD_PAD).astype(bf16)
        gq = row(jnp.concatenate([mla_qn_g[l], mla_qr_g[l], _swap_halves(mla_qr_g[l])]) * MLA_SCALE)
        gk = row(jnp.concatenate([mla_kr_g[l], _swap_halves(mla_kr_g[l])]))
        w_uk = mla_w_uk[l].reshape(MLA_KV_LORA, MLA_HEADS * MLA_NOPE)
        w_uk_b = w_uk.astype(bf16)
        wuk_t = w_uk.T.astype(bf16)
        w_abs = (mla_w_uk[l] * mla_kn_g[l][None, None, :]).transpose(1, 2, 0)
        w_abs = jnp.concatenate([w_abs, jnp.zeros((MLA_HEADS, HEAD_PAD - MLA_NOPE, MLA_KV_LORA), f32)], axis=1).astype(bf16)
        w_uv = mla_w_uv[l].transpose(1, 0, 2).astype(bf16)
        wcat = jnp.zeros((256, 3 * MIX_W), f32)
        wcat = wcat.at[:64, :MIX_W].set(rwkv_w2[l]).at[64:128, MIX_W:2 * MIX_W].set(rwkv_a2[l])
        wcat = wcat.at[128:, 2 * MIX_W:].set(rwkv_g2[l]).astype(bf16)
        rparams = [row(rwkv_mu[l]), wcat, row(rwkv_w0[l]), row(rwkv_a0[l]), row(rwkv_k_k[l]), row(rwkv_k_a[l]),
                   row(rwkv_r_k[l]), ind]

        g_mix = row(norm_mix_g[l])
        proj_a = fmm_norm(x, g_mix, w_a, tm=tm, tn=w_a.shape[1])
        proj_z = fmm_norm(x, g_mix, w_z, tm=tm, tn=RWKV_PROJ // 2)
        proj_g = fmm_norm(x, g_mix, w_g, tm=tm, tn=1536)

        pw = pool_w[l].astype(bf16)
        o_pool = jnp.concatenate([
            pool_prompt(proj_a, pw, row(pool_scale[l]), T=T, tm=tp),
            pool_sample(state_pool[l].transpose(1, 0, 2), proj_a, pw, row(pool_scale[l]), T=T, B=B, past_len=past_len)])
        u = proj_a[:, :MIX_W]
        outs["pool_p"].append(u[T - POOL_HIST:T][None])
        outs["pool_s"].append(jnp.concatenate([state_pool[l][:, 1:], u[T:, None]], axis=1))

        q_full = qprep(proj_a, row(mla_cq_g[l]), w_big, gq, tab, tm=tm)
        lat, lat_b, krope, k_full = kprep(proj_a, row(mla_ckv_g[l]), gk, tab, w_uk_b, row(mla_kn_g[l]), tm=tm)
        outs["lat_p"].append(lat[:T][None])
        outs["lat_s"].append(lat[T:, None])
        outs["rope_p"].append(krope[:T, :MLA_ROPE][None])
        outs["rope_s"].append(krope[T:, None, :MLA_ROPE])
        o_lat_p = attn_prompt(q_full, k_full, lat_b, T=T, tq=tq)
        q_s = q_full[T:]
        qa = gmm(q_s, w_abs, tm=B, out_dtype=bf16).reshape(B, MLA_HEADS, MLA_KV_LORA)
        q_s3 = q_s.reshape(B, MLA_HEADS, HEAD_PAD)
        o_lat_s = attn_sample(page_table, qa, q_s3[:, :, MLA_NOPE:MLA_QK], q_s3,
                              k_full[T:].reshape(B, MLA_HEADS, HEAD_PAD), lat[T:, None], wuk_t,
                              cache_mla_latent, cache_rope_t, layer=l, G=G)
        o_lat = jnp.concatenate([o_lat_p, o_lat_s.reshape(B, MLA_HEADS * MLA_KV_LORA)])
        o_mla = gmm(o_lat, w_uv, tm=tm, out_dtype=bf16)

        (rp, rs_) = rwkv_prep(proj_z, state_rwkv_shift[l], rparams, T=T, B=B, tm=_tile(T, 256))
        y_p, st_t = rwkv_scan(rp[0], rp[1], rp[2], rp[3], rp[4], rp[5], exact=SCAN_EXACT)
        s_new, y_s = rwkv_step(state_rwkv[l], rs_[0], rs_[1], rs_[2], rs_[3], rs_[4], rs_[5])
        outs["rwkv_p"].append(st_t.transpose(0, 2, 1)[None])
        outs["rwkv_s"].append(s_new)
        outs["shift_p"].append(proj_z[T - 1:T])
        outs["shift_s"].append(proj_z[T:])
        cat = lambda a, b: jnp.concatenate([a, b])
        o_rwkv = rwkv_post(cat(y_p, y_s), cat(rp[7], rs_[7]), cat(rp[6], rs_[6]),
                           row(rwkv_ln_w[l]), row(rwkv_ln_b[l]), ind, tm=tm)

        merged = merge((o_pool, o_mla, o_rwkv), w_branch[l].astype(bf16), proj_g, tm=tm, tn=512)
        x = fmm(merged, w_out[l].astype(bf16), tm=tm, tn=1024, tk=D_MODEL, res=x)

        w_kv = jnp.concatenate([w_k_mem[l], w_v_mem[l]], axis=1).astype(bf16)
        n_mem = mem_prompt.shape[1]
        kv = fmm_norm(mem_prompt[0], row(mem_norm_g[l]), w_kv, tm=n_mem, tn=2 * MEM_W)
        mem_k, mem_v = memkv_post(kv, row(mem_kn_g[l]))
        outs["memk_p"].append(mem_k.reshape(1, n_mem, MEM_HEADS, MEM_HEAD_DIM))
        outs["memv_p"].append(mem_v.reshape(1, n_mem, MEM_HEADS, MEM_HEAD_DIM))
        q_mem = fmm_norm(x, row(norm_mem_g[l]), w_q_mem[l].astype(bf16), tm=tm, tn=MEM_W)
        o_mem = jnp.concatenate([
            memattn_prompt(q_mem, row(mem_qn_g[l]), mem_k, mem_v, T=T, tm=tp),
            memattn_sample(q_mem, row(mem_qn_g[l]), cache_mem_k, cache_mem_v, layer=l, T=T, B=B, tb=tb)])
        x = fmm(o_mem, w_o_mem[l].astype(bf16), tm=tm, tn=1024, tk=MEM_W, res=x)

        g_ffn = row(norm_ffn_g[l])
        if l % 2 == 0:
            d = l // 2
            a = ffn1(x, g_ffn, ffn_w_gate[d].astype(bf16)[None], ffn_w_up[d].astype(bf16)[None], tm=tm, tn=1408)
            x = fmm(a, ffn_w_down[d].astype(bf16), tm=tm, tn=1024, tk=1408, res=x)
        else:
            e = l // 2
            w_r = jnp.pad(moe_router[e], ((0, 0), (0, 128 - N_EXPERTS)))
            b_r = row(jnp.pad(moe_router_b[e], (0, 128 - N_EXPERTS)))
            comb = router(x, g_ffn, w_r, b_r, expand, tm=tm)
            fe = moe_w_gate.shape[-1]
            wd = moe_w_down[e].reshape(N_EXPERTS * fe, D_MODEL).astype(bf16)
            a = ffn1(x, g_ffn, moe_w_gate[e].astype(bf16), moe_w_up[e].astype(bf16), tm=tm, tn=1408, comb=comb)
            x = fmm(a, wd, tm=tm, tn=1024, tk=fe, res=x)

    st = lambda k: jnp.stack(outs[k])
    return (x[:T][None], x[T:, None],
            st("pool_p"), st("lat_p"), st("rope_p"), st("rwkv_p"), st("shift_p"), st("memk_p"), st("memv_p"),
            st("pool_s"), st("lat_s"), st("rope_s"), st("rwkv_s"), st("shift_s"))
```

```python
import functools

import jax
import jax.numpy as jnp
from jax import lax
from jax.experimental import pallas as pl
from jax.experimental.pallas import tpu as pltpu

f32 = jnp.float32
bf16 = jnp.bfloat16

D_MODEL = 2048
MIX_W = 1024
POOL_WINDOWS = (2, 4, 8, 16)
POOL_GW = 256
POOL_HIST = 15
MLA_HEADS = 8
MLA_NOPE = 128
MLA_ROPE = 64
MLA_QK = MLA_NOPE + MLA_ROPE
MLA_Q_LORA = 512
MLA_KV_LORA = 256
MLA_SCALE = MLA_QK ** -0.5
ROPE_THETA = 10000.0
PAGE_SIZE = 128
RWKV_HEAD = 64
RWKV_HEADS = 16
RWKV_PROJ = 3328
MEM_HEADS = 4
MEM_HEAD_DIM = 128
MEM_W = 512
N_EXPERTS = 8
RMS_EPS = 1e-6
GN_EPS = 64e-5
NEG_INF = -1e30
OFF_CQ, OFF_CKV, OFF_KR, OFF_RWKV, OFF_GATE = 1024, 1536, 1792, 1856, 5184
HEAD_PAD = 256
PAGE_SLOTS = 3
HEAD_GROUP = 4
SCAN_CHUNK = 64
VMEM_LIMIT = 50 * 1024 * 1024
HI = lax.Precision.HIGHEST
SCAN_EXACT = False


def _cp(*sem, vmem=VMEM_LIMIT):
    return pltpu.CompilerParams(dimension_semantics=sem, vmem_limit_bytes=vmem)


def _tile(n, pref, mult=8):
    best = None
    t = mult
    while t <= min(n, pref):
        if n % t == 0:
            best = t
        t += mult
    return best if best is not None else n


def _nt(a, b, precision=None):
    return lax.dot_general(a, b, (((1,), (1,)), ((), ())), precision=precision,
                           preferred_element_type=f32)


def _mm(a, b, precision=None):
    return jnp.dot(a, b, precision=precision, preferred_element_type=f32)


def _rms(x, g):
    return x * lax.rsqrt(jnp.mean(x * x, axis=-1, keepdims=True) + RMS_EPS) * g


def _segsum(x, ind):
    hi = x.astype(bf16)
    lo = (x - hi.astype(f32)).astype(bf16)
    return _mm(hi, ind) + _mm(lo, ind)


def _fmm_norm_kernel(x_ref, g_ref, w_ref, *rest, has_res):
    if has_res:
        r_ref, o_ref, xn_ref = rest
    else:
        o_ref, xn_ref = rest

    @pl.when(pl.program_id(1) == 0)
    def _():
        xn_ref[...] = _rms(x_ref[...], g_ref[...]).astype(bf16)

    acc = _mm(xn_ref[...], w_ref[...])
    if has_res:
        acc = acc + r_ref[...]
    o_ref[...] = acc.astype(o_ref.dtype)


def fmm_norm(x, g, w, *, tm, tn, out_dtype=f32, res=None):
    M, K = x.shape
    N = w.shape[1]
    in_specs = [pl.BlockSpec((tm, K), lambda i, j: (i, 0)),
                pl.BlockSpec((1, K), lambda i, j: (0, 0)),
                pl.BlockSpec((K, tn), lambda i, j: (0, j))]
    args = [x, g, w]
    if res is not None:
        in_specs.append(pl.BlockSpec((tm, tn), lambda i, j: (i, j)))
        args.append(res)
    return pl.pallas_call(
        functools.partial(_fmm_norm_kernel, has_res=res is not None),
        grid=(M // tm, N // tn),
        in_specs=in_specs,
        out_specs=pl.BlockSpec((tm, tn), lambda i, j: (i, j)),
        out_shape=jax.ShapeDtypeStruct((M, N), out_dtype),
        scratch_shapes=[pltpu.VMEM((tm, K), bf16)],
        compiler_params=_cp("parallel", "arbitrary"),
        name="fmm_norm",
    )(*args)


def _fmm_kernel(x_ref, w_ref, *rest, has_res, nk):
    if has_res:
        r_ref, o_ref, acc_ref = rest
    else:
        o_ref, acc_ref = rest
    k = pl.program_id(2)

    @pl.when(k == 0)
    def _():
        acc_ref[...] = jnp.zeros_like(acc_ref)

    acc_ref[...] += _mm(x_ref[...].astype(bf16), w_ref[...])

    @pl.when(k == nk - 1)
    def _():
        a = acc_ref[...]
        if has_res:
            a = a + r_ref[...]
        o_ref[...] = a.astype(o_ref.dtype)


def fmm(x, w, *, tm, tn, tk, out_dtype=f32, res=None):
    M, K = x.shape
    N = w.shape[1]
    nk = K // tk
    in_specs = [pl.BlockSpec((tm, tk), lambda i, j, k: (i, k)),
                pl.BlockSpec((tk, tn), lambda i, j, k: (k, j))]
    args = [x, w]
    if res is not None:
        in_specs.append(pl.BlockSpec((tm, tn), lambda i, j, k: (i, j)))
        args.append(res)
    return pl.pallas_call(
        functools.partial(_fmm_kernel, has_res=res is not None, nk=nk),
        grid=(M // tm, N // tn, nk),
        in_specs=in_specs,
        out_specs=pl.BlockSpec((tm, tn), lambda i, j, k: (i, j)),
        out_shape=jax.ShapeDtypeStruct((M, N), out_dtype),
        scratch_shapes=[pltpu.VMEM((tm, tn), f32)],
        compiler_params=_cp("parallel", "parallel", "arbitrary"),
        name="fmm",
    )(*args)


def _gmm_kernel(x_ref, w_ref, o_ref):
    o_ref[...] = _mm(x_ref[...].astype(bf16), w_ref[...]).astype(o_ref.dtype)


def gmm(x, w, *, tm, out_dtype=f32):
    M = x.shape[0]
    G, K, N = w.shape
    return pl.pallas_call(
        _gmm_kernel,
        grid=(M // tm, G),
        in_specs=[pl.BlockSpec((tm, K), lambda i, g: (i, g)),
                  pl.BlockSpec((None, K, N), lambda i, g: (g, 0, 0))],
        out_specs=pl.BlockSpec((tm, N), lambda i, g: (i, g)),
        out_shape=jax.ShapeDtypeStruct((M, G * N), out_dtype),
        compiler_params=_cp("parallel", "arbitrary"),
        name="gmm",
    )(x, w)


def _pool_p_kernel(u_ref, halo_ref, w_ref, sc_ref, o_ref, ext_ref, *, tm):
    i = pl.program_id(0)
    ext_ref[0:16, :] = jnp.where(i > 0, halo_ref[...], 0.0)
    ext_ref[16:, :] = u_ref[...]
    pos = i * tm + lax.broadcasted_iota(jnp.int32, (tm, 1), 0)
    for g, w in enumerate(POOL_WINDOWS):
        cols = slice(g * POOL_GW, (g + 1) * POOL_GW)
        s = ext_ref[16:, cols]
        for d in range(1, w):
            s = s + ext_ref[pl.ds(16 - d, tm), cols]
        pooled = s / jnp.minimum(pos + 1, w).astype(f32)
        diff = (pooled - ext_ref[16:, cols]).astype(bf16)
        o_ref[:, cols] = (_mm(diff, w_ref[g]) * sc_ref[:, cols]).astype(o_ref.dtype)


def pool_prompt(proj_a, pool_w, scale, *, T, tm):
    return pl.pallas_call(
        functools.partial(_pool_p_kernel, tm=tm),
        grid=(T // tm,),
        in_specs=[pl.BlockSpec((tm, MIX_W), lambda i: (i, 0)),
                  pl.BlockSpec((16, MIX_W), lambda i: (jnp.maximum(i * (tm // 16) - 1, 0), 0)),
                  pl.BlockSpec((4, POOL_GW, POOL_GW), lambda i: (0, 0, 0)),
                  pl.BlockSpec((1, MIX_W), lambda i: (0, 0))],
        out_specs=pl.BlockSpec((tm, MIX_W), lambda i: (i, 0)),
        out_shape=jax.ShapeDtypeStruct((T, MIX_W), bf16),
        scratch_shapes=[pltpu.VMEM((tm + 16, MIX_W), f32)],
        compiler_params=_cp("parallel"),
        name="pool_prompt",
    )(proj_a, proj_a, pool_w, scale)


def _pool_s_kernel(h_ref, u_ref, w_ref, sc_ref, o_ref, *, counts):
    for g, w in enumerate(POOL_WINDOWS):
        cols = slice(g * POOL_GW, (g + 1) * POOL_GW)
        u = u_ref[:, cols]
        s = u
        for d in range(1, w):
            s = s + h_ref[POOL_HIST - d, :, cols]
        diff = (s / counts[g] - u).astype(bf16)
        o_ref[:, cols] = (_mm(diff, w_ref[g]) * sc_ref[:, cols]).astype(o_ref.dtype)


def pool_sample(hist_t, proj_a, pool_w, scale, *, T, B, past_len):
    counts = tuple(float(min(past_len + 1, w)) for w in POOL_WINDOWS)
    return pl.pallas_call(
        functools.partial(_pool_s_kernel, counts=counts),
        grid=(1,),
        in_specs=[pl.BlockSpec((POOL_HIST, B, MIX_W), lambda i: (0, 0, 0)),
                  pl.BlockSpec((B, MIX_W), lambda i: (T // B, 0)),
                  pl.BlockSpec((4, POOL_GW, POOL_GW), lambda i: (0, 0, 0)),
                  pl.BlockSpec((1, MIX_W), lambda i: (0, 0))],
        out_specs=pl.BlockSpec((B, MIX_W), lambda i: (0, 0)),
        out_shape=jax.ShapeDtypeStruct((B, MIX_W), bf16),
        compiler_params=_cp("arbitrary"),
        name="pool_sample",
    )(hist_t, proj_a, pool_w, scale)


def _rope_half(hi, coef):
    rs = lax.rsqrt(jnp.sum(hi * hi, axis=-1, keepdims=True) * (0.5 / MLA_ROPE) + RMS_EPS)
    t = hi * rs * coef
    r = t + pltpu.roll(t, 64, 1)
    lane = lax.broadcasted_iota(jnp.int32, (1, 128), 1)
    return jnp.where(lane < MLA_ROPE, r, 0.0)


def _qprep_kernel(c_ref, g_ref, w_ref, gq_ref, tab_ref, o_ref):
    cn = _rms(c_ref[...], g_ref[...]).astype(bf16)
    x = _mm(cn, w_ref[...])
    gq = gq_ref[...]
    coef = tab_ref[...] * gq[:, MLA_NOPE:]
    for h in range(MLA_HEADS):
        b = h * HEAD_PAD
        o_ref[:, b:b + MLA_NOPE] = _rms(x[:, b:b + MLA_NOPE], gq[:, :MLA_NOPE]).astype(o_ref.dtype)
        o_ref[:, b + MLA_NOPE:b + HEAD_PAD] = _rope_half(x[:, b + MLA_NOPE:b + HEAD_PAD], coef).astype(o_ref.dtype)


def qprep(proj_a, cq_g, w_big, gq, tab, *, tm):
    M = proj_a.shape[0]
    return pl.pallas_call(
        _qprep_kernel,
        grid=(M // tm,),
        in_specs=[pl.BlockSpec((tm, MLA_Q_LORA), lambda i: (i, OFF_CQ // MLA_Q_LORA)),
                  pl.BlockSpec((1, MLA_Q_LORA), lambda i: (0, 0)),
                  pl.BlockSpec((MLA_Q_LORA, MLA_HEADS * HEAD_PAD), lambda i: (0, 0)),
                  pl.BlockSpec((1, HEAD_PAD), lambda i: (0, 0)),
                  pl.BlockSpec((tm, 128), lambda i: (i, 0))],
        out_specs=pl.BlockSpec((tm, MLA_HEADS * HEAD_PAD), lambda i: (i, 0)),
        out_shape=jax.ShapeDtypeStruct((M, MLA_HEADS * HEAD_PAD), bf16),
        compiler_params=_cp("parallel"),
        name="qprep",
    )(proj_a, cq_g, w_big, gq, tab)


def _kprep_kernel(c_ref, g_ref, gk_ref, tab_ref, w_ref, kng_ref, lat_ref, latb_ref, kr_ref, k_ref):
    ck = c_ref[...]
    lat = _rms(ck[:, :MLA_KV_LORA], g_ref[...])
    lat_ref[...] = lat
    latb = lat.astype(bf16)
    latb_ref[...] = latb
    kr = _rope_half(ck[:, MLA_KV_LORA:], tab_ref[...] * gk_ref[...])
    kr_ref[...] = kr
    krb = kr.astype(bf16)
    kn = _mm(latb, w_ref[...])
    for h in range(MLA_HEADS):
        b = h * HEAD_PAD
        k_ref[:, b:b + MLA_NOPE] = _rms(kn[:, h * MLA_NOPE:(h + 1) * MLA_NOPE], kng_ref[...]).astype(bf16)
        k_ref[:, b + MLA_NOPE:b + HEAD_PAD] = krb


def kprep(proj_a, ckv_g, gk, tab, w_uk, kn_g, *, tm):
    M = proj_a.shape[0]
    wk = MLA_KV_LORA + 128
    row = lambda n: pl.BlockSpec((tm, n), lambda i: (i, 0))
    return pl.pallas_call(
        _kprep_kernel,
        grid=(M // tm,),
        in_specs=[pl.BlockSpec((tm, wk), lambda i: (i, OFF_CKV // wk)),
                  pl.BlockSpec((1, MLA_KV_LORA), lambda i: (0, 0)),
                  pl.BlockSpec((1, 128), lambda i: (0, 0)),
                  row(128),
                  pl.BlockSpec((MLA_KV_LORA, MLA_HEADS * MLA_NOPE), lambda i: (0, 0)),
                  pl.BlockSpec((1, MLA_NOPE), lambda i: (0, 0))],
        out_specs=[row(MLA_KV_LORA), row(MLA_KV_LORA), row(128), row(MLA_HEADS * HEAD_PAD)],
        out_shape=[jax.ShapeDtypeStruct((M, MLA_KV_LORA), f32),
                   jax.ShapeDtypeStruct((M, MLA_KV_LORA), bf16),
                   jax.ShapeDtypeStruct((M, 128), f32),
                   jax.ShapeDtypeStruct((M, MLA_HEADS * HEAD_PAD), bf16)],
        compiler_params=_cp("parallel"),
        name="kprep",
    )(proj_a, ckv_g, gk, tab, w_uk, kn_g)


def _attn_p_kernel(q_ref, k_ref, v_ref, o_ref, m_ref, l_ref, acc_ref, *, tq):
    qi = pl.program_id(0)
    ki = pl.program_id(1)

    @pl.when(ki == 0)
    def _():
        m_ref[...] = jnp.full_like(m_ref, NEG_INF)
        l_ref[...] = jnp.zeros_like(l_ref)
        acc_ref[...] = jnp.zeros_like(acc_ref)

    def step(masked):
        if masked:
            rows = lax.broadcasted_iota(jnp.int32, (tq, tq), 0)
            cols = lax.broadcasted_iota(jnp.int32, (tq, tq), 1)
            bias = jnp.where(cols <= rows, 0.0, NEG_INF)
        v = v_ref[...]
        rep = tq // 128
        for h0 in range(0, MLA_HEADS, HEAD_GROUP):
            H = range(h0, h0 + HEAD_GROUP)
            hs = {h: slice(h * HEAD_PAD, (h + 1) * HEAD_PAD) for h in H}
            s = {h: _nt(q_ref[:, hs[h]], k_ref[:, hs[h]]) for h in H}
            if masked:
                s = {h: s[h] + bias for h in H}
            m_prev = {h: m_ref[h] for h in H}
            m_new = {h: jnp.maximum(m_prev[h], jnp.max(s[h], axis=1, keepdims=True)) for h in H}
            p = {h: jnp.exp(s[h] - jnp.concatenate([m_new[h]] * rep, axis=1)) for h in H}
            alpha = {h: jnp.exp(m_prev[h] - m_new[h]) for h in H}
            pv = {h: _mm(p[h].astype(bf16), v) for h in H}
            for h in H:
                l_ref[h] = alpha[h] * l_ref[h] + jnp.sum(p[h], axis=1, keepdims=True)
                acc_ref[h] = jnp.concatenate([alpha[h]] * (MLA_KV_LORA // 128), axis=1) * acc_ref[h] + pv[h]
                m_ref[h] = m_new[h]

    @pl.when(ki < qi)
    def _():
        step(False)

    @pl.when(ki == qi)
    def _():
        step(True)

    @pl.when(ki == qi)
    def _():
        for h in range(MLA_HEADS):
            hs = slice(h * HEAD_PAD, (h + 1) * HEAD_PAD)
            o_ref[:, hs] = (acc_ref[h] / l_ref[h][:, :1]).astype(o_ref.dtype)


def attn_prompt(q_full, k_full, lat_b, *, T, tq):
    n = T // tq
    W = MLA_HEADS * HEAD_PAD
    return pl.pallas_call(
        functools.partial(_attn_p_kernel, tq=tq),
        grid=(n, n),
        in_specs=[pl.BlockSpec((tq, W), lambda qi, ki: (qi, 0)),
                  pl.BlockSpec((tq, W), lambda qi, ki: (jnp.minimum(ki, qi), 0)),
                  pl.BlockSpec((tq, MLA_KV_LORA), lambda qi, ki: (jnp.minimum(ki, qi), 0))],
        out_specs=pl.BlockSpec((tq, W), lambda qi, ki: (qi, 0)),
        out_shape=jax.ShapeDtypeStruct((T, W), bf16),
        scratch_shapes=[pltpu.VMEM((MLA_HEADS, tq, 128), f32),
                        pltpu.VMEM((MLA_HEADS, tq, 128), f32),
                        pltpu.VMEM((MLA_HEADS, tq, MLA_KV_LORA), f32)],
        compiler_params=_cp("parallel", "arbitrary"),
        name="attn_prompt",
    )(q_full, k_full, lat_b)


def _attn_s_kernel(pt_ref, qa_ref, qr_ref, qf_ref, kn_ref, ln_ref, wt_ref, lat_hbm, rope_hbm, o_ref,
                   lat_buf, rope_buf, sem, m_ref, l_ref, acc_ref, *, G, ng, layer, nsteps):
    b = pl.program_id(0)
    ahead = PAGE_SLOTS - 1

    def page_copies(step, slot):
        out = []
        for i in range(G):
            page = pt_ref[step * G + i]
            out.append(pltpu.make_async_copy(lat_hbm.at[layer, page], lat_buf.at[slot, i], sem.at[slot]))
            out.append(pltpu.make_async_copy(rope_hbm.at[layer, page], rope_buf.at[slot, i], sem.at[slot]))
        return out

    @pl.when(b == 0)
    def _():
        for k in range(ahead):
            for c in page_copies(jnp.minimum(k, nsteps - 1), k):
                c.start()

    m_ref[...] = jnp.full_like(m_ref, NEG_INF)
    l_ref[...] = jnp.zeros_like(l_ref)
    acc_ref[...] = jnp.zeros_like(acc_ref)

    def step(g, carry):
        s = b * ng + g
        slot = lax.rem(s, PAGE_SLOTS)
        for c in page_copies(s, slot):
            c.wait()
        qa = qa_ref[...]
        qr = qr_ref[...]
        wt = wt_ref[...]
        lat_all = jnp.concatenate([lat_buf[slot, i].astype(bf16) for i in range(G)], axis=0)
        pairs = range(0, G, 2)
        lat = [lat_all[i * PAGE_SIZE:(i + 2) * PAGE_SIZE] for i in pairs]
        rope_t = [jnp.concatenate([rope_buf[slot, i], rope_buf[slot, i + 1]], axis=1).astype(bf16) for i in pairs]
        kt = [_nt(wt, x) for x in lat]
        num = [_nt(qa, x) for x in lat]
        rr = [_mm(qr, x) for x in rope_t]
        ssq = [jnp.sum((x * x).reshape(MLA_HEADS, MLA_NOPE, 2 * PAGE_SIZE), axis=1) for x in kt]
        parts = [n * lax.rsqrt(q * (1.0 / MLA_NOPE) + RMS_EPS) + r for n, q, r in zip(num, ssq, rr)]
        sc = jnp.concatenate(parts, axis=1)
        m_prev = m_ref[...]
        m_new = jnp.maximum(m_prev, jnp.max(sc, axis=1, keepdims=True))
        alpha = jnp.exp(m_prev - m_new)
        p = jnp.exp(sc - m_new[:, :1])
        l_ref[...] = alpha * l_ref[...] + jnp.sum(p, axis=1, keepdims=True)
        acc_ref[...] = alpha[:, :1] * acc_ref[...] + _mm(p.astype(bf16), lat_all)
        m_ref[...] = m_new
        for c in page_copies(jnp.minimum(s + ahead, nsteps - 1), lax.rem(s + ahead, PAGE_SLOTS)):
            c.start()
        return carry

    lax.fori_loop(0, ng, step, 0)

    s_new = jnp.sum(qf_ref[...].astype(f32) * kn_ref[...].astype(f32), axis=1, keepdims=True)
    m_prev = m_ref[...]
    m_new = jnp.maximum(m_prev, s_new)
    alpha = jnp.exp(m_prev - m_new)
    p = jnp.exp(s_new - m_new)
    l = alpha * l_ref[...] + p
    acc = alpha[:, :1] * acc_ref[...] + p[:, :1] * ln_ref[...].astype(bf16).astype(f32)
    o_ref[...] = (acc / l[:, :1]).astype(o_ref.dtype)

    @pl.when(b == pl.num_programs(0) - 1)
    def _():
        for k in range(ahead):
            for c in page_copies(nsteps - 1, lax.rem(nsteps + k, PAGE_SLOTS)):
                c.wait()


def attn_sample(page_table, qa, qr, qf, knew, latnew, wuk_t, cache_lat, cache_rope_t, *, layer, G):
    B, n_pages = page_table.shape
    ng = n_pages // G
    pt = page_table.reshape(-1)
    per_b = lambda *shape: pl.BlockSpec((None,) + shape, lambda b, pt: (b,) + (0,) * len(shape))
    in_specs = [per_b(MLA_HEADS, MLA_KV_LORA), per_b(MLA_HEADS, MLA_ROPE), per_b(MLA_HEADS, HEAD_PAD),
                per_b(MLA_HEADS, HEAD_PAD), per_b(1, MLA_KV_LORA),
                pl.BlockSpec((MLA_HEADS * MLA_NOPE, MLA_KV_LORA), lambda b, pt: (0, 0)),
                pl.BlockSpec(memory_space=pl.ANY), pl.BlockSpec(memory_space=pl.ANY)]
    return pl.pallas_call(
        functools.partial(_attn_s_kernel, G=G, ng=ng, layer=layer, nsteps=B * ng),
        grid_spec=pltpu.PrefetchScalarGridSpec(
            num_scalar_prefetch=1,
            grid=(B,),
            in_specs=in_specs,
            out_specs=per_b(MLA_HEADS, MLA_KV_LORA),
            scratch_shapes=[pltpu.VMEM((PAGE_SLOTS, G, PAGE_SIZE, MLA_KV_LORA), f32),
                            pltpu.VMEM((PAGE_SLOTS, G, MLA_ROPE, PAGE_SIZE), f32),
                            pltpu.SemaphoreType.DMA((PAGE_SLOTS,)),
                            pltpu.VMEM((MLA_HEADS, 128), f32),
                            pltpu.VMEM((MLA_HEADS, 128), f32),
                            pltpu.VMEM((MLA_HEADS, MLA_KV_LORA), f32)]),
        out_shape=jax.ShapeDtypeStruct((B, MLA_HEADS, MLA_KV_LORA), bf16),
        compiler_params=_cp("arbitrary"),
        name="attn_sample",
    )(pt, qa, qr, qf, knew, latnew, wuk_t, cache_lat, cache_rope_t)


def _rwkv_prep_math(z, zs, mu_ref, wcat_ref, w0_ref, a0_ref, kk_ref, ka_ref, rk_ref, ind_ref, outs):
    r_o, k_o, v_o, kk_o, b_o, lw_o, g_o, bonus_o = outs
    zm = z + (zs - z) * mu_ref[...]
    r = zm[:, :MIX_W]
    k = zm[:, MIX_W:2 * MIX_W]
    v = zm[:, 2 * MIX_W:3 * MIX_W]
    tail = zm[:, 3 * MIX_W:]
    lane = lax.broadcasted_iota(jnp.int32, (1, 256), 1)
    act = jnp.where(lane < 64, jnp.tanh(tail), jnp.where(lane < 128, tail, 1.0 / (1.0 + jnp.exp(-tail))))
    lo = _mm(act.astype(bf16), wcat_ref[...])
    u = -(w0_ref[...] + lo[:, :MIX_W])
    w = -(jnp.maximum(u, 0.0) + jnp.log(1.0 + jnp.exp(-jnp.abs(u)))) - 0.5
    a = 1.0 / (1.0 + jnp.exp(-(a0_ref[...] + lo[:, MIX_W:2 * MIX_W])))
    ind = ind_ref[...]
    kk = k * kk_ref[...]
    kk = kk * lax.rsqrt(_segsum(kk * kk, ind) + 1e-12)
    k_eff = k * (1.0 + (a - 1.0) * ka_ref[...])
    r_o[...] = r
    k_o[...] = k_eff
    v_o[...] = v
    kk_o[...] = kk
    b_o[...] = kk * a
    lw_o[...] = -jnp.exp(w)
    g_o[...] = lo[:, 2 * MIX_W:]
    bonus_o[...] = _segsum(r * k_eff * rk_ref[...], ind) * v


def _rwkv_prep_p_kernel(z_ref, halo_ref, *rest):
    params, outs = rest[:8], rest[8:]
    z = z_ref[...]
    first = pl.program_id(0) == 0
    prev = jnp.where(first, 0.0, halo_ref[7:8, :])
    row = lax.broadcasted_iota(jnp.int32, (z.shape[0], 1), 0)
    zs = jnp.where(row == 0, prev, pltpu.roll(z, 1, 0))
    _rwkv_prep_math(z, zs, *params, outs)


def _rwkv_prep_s_kernel(z_ref, zs_ref, *rest):
    params, outs = rest[:8], rest[8:]
    _rwkv_prep_math(z_ref[...], zs_ref[...], *params, outs)


def rwkv_prep(proj_z, shift, params, *, T, B, tm):
    const = lambda a: pl.BlockSpec(a.shape, lambda i: (0,) * a.ndim)
    pspecs = [const(p) for p in params]
    outs_p = [jax.ShapeDtypeStruct((T, MIX_W), f32)] * 8
    res_p = pl.pallas_call(
        _rwkv_prep_p_kernel,
        grid=(T // tm,),
        in_specs=[pl.BlockSpec((tm, RWKV_PROJ), lambda i: (i, 0)),
                  pl.BlockSpec((8, RWKV_PROJ), lambda i: (jnp.maximum(i * (tm // 8) - 1, 0), 0))] + pspecs,
        out_specs=[pl.BlockSpec((tm, MIX_W), lambda i: (i, 0))] * 8,
        out_shape=outs_p,
        compiler_params=_cp("parallel"),
        name="rwkv_prep_prompt",
    )(proj_z, proj_z, *params)
    res_s = pl.pallas_call(
        _rwkv_prep_s_kernel,
        grid=(1,),
        in_specs=[pl.BlockSpec((B, RWKV_PROJ), lambda i: (T // B, 0)),
                  pl.BlockSpec((B, RWKV_PROJ), lambda i: (0, 0))] + pspecs,
        out_specs=[pl.BlockSpec((B, MIX_W), lambda i: (0, 0))] * 8,
        out_shape=[jax.ShapeDtypeStruct((B, MIX_W), f32)] * 8,
        compiler_params=_cp("arbitrary"),
        name="rwkv_prep_sample",
    )(proj_z, shift, *params)
    return res_p, res_s


def _scan_kernel(r_ref, k_ref, v_ref, kk_ref, b_ref, lw_ref, lwt_ref, bt_ref, kt_ref,
                 y_ref, so_ref, st_ref, *, exact):
    C = SCAN_CHUNK
    N = RWKV_HEAD

    @pl.when(pl.program_id(0) == 0)
    def _():
        st_ref[...] = jnp.zeros_like(st_ref)

    row = lax.broadcasted_iota(jnp.int32, (C, C), 0)
    col = lax.broadcasted_iota(jnp.int32, (C, C), 1)
    strict = col < row
    incl = col <= row
    eye = (row == col).astype(f32)
    if exact:
        op = lambda x: x
        mm = functools.partial(_mm, precision=HI)
        nt = functools.partial(_nt, precision=HI)
    else:
        op = lambda x: x.astype(bf16)
        mm, nt = _mm, _nt

    lw = lw_ref[...]
    cs = _mm(incl.astype(f32), lw, precision=HI)
    cst = _mm(lwt_ref[...], (row <= col).astype(f32), precision=HI)
    em = jnp.exp(-cs)
    emt = jnp.exp(-cst)
    ar_all = op(jnp.concatenate([-kk_ref[...] * jnp.exp(cs - lw), r_ref[...] * jnp.exp(cs)], axis=0))
    bh_all = op(b_ref[...] * em)
    kh_all = op(k_ref[...] * em)
    bkt_all = op(jnp.concatenate([bt_ref[...] * emt, kt_ref[...] * emt], axis=1))
    g_all = jnp.exp(cst[:, C - 1:C])
    v_all = op(v_ref[...])
    H = range(RWKV_HEADS)
    hsl = [slice(h * N, (h + 1) * N) for h in H]
    ar = [ar_all[:, s] for s in hsl]
    v = [v_all[:, s] for s in hsl]
    pb = [nt(ar[h], bh_all[:, hsl[h]]) for h in H]
    pk = [nt(ar[h], kh_all[:, hsl[h]]) for h in H]
    pw = [jnp.where(strict, pb[h][:C], 0.0) for h in H]
    tinv = [eye + pw[h] for h in H]
    for _ in range(5):
        pw = [mm(op(pw[h]), op(pw[h])) for h in H]
        tinv = [tinv[h] + mm(op(tinv[h]), op(pw[h])) for h in H]
    st = [st_ref[h] for h in H]
    xs = [mm(ar[h], op(st[h])) for h in H]
    pv = [mm(op(jnp.concatenate([jnp.where(strict, pk[h][:C], 0.0), jnp.where(incl, pk[h][C:], 0.0)], axis=0)), v[h])
          for h in H]
    ut = [mm(op(tinv[h]), op(xs[h][:C] + pv[h][:C])) for h in H]
    for h in H:
        y_ref[:, hsl[h]] = xs[h][C:] + pv[h][C:] + mm(op(jnp.where(incl, pb[h][C:], 0.0)), op(ut[h]))
    for h in H:
        uv = jnp.concatenate([op(ut[h]), v[h]], axis=0)
        st_ref[h] = g_all[hsl[h]] * (st[h] + mm(bkt_all[hsl[h]], uv))
    so_ref[...] = st_ref[...]


def rwkv_scan(r, k, v, kk, b, lw, *, exact):
    T = r.shape[0]
    C = SCAN_CHUNK
    nc = T // C
    tr = lambda x: x.reshape(nc, C, MIX_W).transpose(0, 2, 1)
    row = pl.BlockSpec((C, MIX_W), lambda c: (c, 0))
    colm = pl.BlockSpec((None, MIX_W, C), lambda c: (c, 0, 0))
    return pl.pallas_call(
        functools.partial(_scan_kernel, exact=exact),
        grid=(nc,),
        in_specs=[row] * 6 + [colm] * 3,
        out_specs=[row, pl.BlockSpec((RWKV_HEADS, RWKV_HEAD, RWKV_HEAD), lambda c: (0, 0, 0))],
        out_shape=[jax.ShapeDtypeStruct((T, MIX_W), f32),
                   jax.ShapeDtypeStruct((RWKV_HEADS, RWKV_HEAD, RWKV_HEAD), f32)],
        scratch_shapes=[pltpu.VMEM((RWKV_HEADS, RWKV_HEAD, RWKV_HEAD), f32)],
        compiler_params=_cp("arbitrary"),
        name="rwkv_scan",
    )(r, k, v, kk, b, lw, tr(lw), tr(b), tr(k))


def _rwkv_step_kernel(s_ref, r_ref, k_ref, v_ref, kk_ref, b_ref, lw_ref, so_ref, y_ref):
    s = s_ref[...]
    sa = jnp.sum(s * (-kk_ref[...])[None], axis=1, keepdims=True)
    s_new = s * jnp.exp(lw_ref[...])[None] + sa * b_ref[...][None] + v_ref[...] * k_ref[...][None]
    so_ref[...] = s_new
    y_ref[...] = jnp.sum(s_new * r_ref[...][None], axis=1, keepdims=True)


def rwkv_step(state, r, k, v, kk, b, lw):
    B = state.shape[0]
    H, N = RWKV_HEADS, RWKV_HEAD
    st = state.transpose(1, 2, 3, 0)
    keyv = lambda x: x.T.reshape(H, N, B)
    kspec = pl.BlockSpec((None, N, B), lambda h: (h, 0, 0))
    cspec = pl.BlockSpec((None, N, 1, B), lambda h: (h, 0, 0, 0))
    sspec = pl.BlockSpec((None, N, N, B), lambda h: (h, 0, 0, 0))
    s_new, y = pl.pallas_call(
        _rwkv_step_kernel,
        grid=(H,),
        in_specs=[sspec, kspec, kspec, cspec, kspec, kspec, kspec],
        out_specs=[sspec, cspec],
        out_shape=[jax.ShapeDtypeStruct((H, N, N, B), f32), jax.ShapeDtypeStruct((H, N, 1, B), f32)],
        compiler_params=_cp("parallel"),
        name="rwkv_step",
    )(st, keyv(r), keyv(k), keyv(v).reshape(H, N, 1, B), keyv(kk), keyv(b), keyv(lw))
    return s_new.transpose(3, 0, 1, 2), y.reshape(H * N, B).T


def _rwkv_post_kernel(y_ref, bonus_ref, g_ref, lnw_ref, lnb_ref, ind_ref, o_ref):
    ind = ind_ref[...]
    y = y_ref[...]
    d = y - _segsum(y, ind) * (1.0 / RWKV_HEAD)
    var = _segsum(d * d, ind) * (1.0 / RWKV_HEAD)
    yn = d * lax.rsqrt(var + GN_EPS) * lnw_ref[...] + lnb_ref[...]
    o_ref[...] = ((yn + bonus_ref[...]) * g_ref[...]).astype(o_ref.dtype)


def rwkv_post(y, bonus, g, ln_w, ln_b, ind, *, tm):
    M = y.shape[0]
    row = pl.BlockSpec((tm, MIX_W), lambda i: (i, 0))
    vec = pl.BlockSpec((1, MIX_W), lambda i: (0, 0))
    return pl.pallas_call(
        _rwkv_post_kernel,
        grid=(M // tm,),
        in_specs=[row, row, row, vec, vec, pl.BlockSpec((MIX_W, MIX_W), lambda i: (0, 0))],
        out_specs=row,
        out_shape=jax.ShapeDtypeStruct((M, MIX_W), bf16),
        compiler_params=_cp("parallel"),
        name="rwkv_post",
    )(y, bonus, g, ln_w, ln_b, ind)


def _merge_kernel(b0_ref, b1_ref, b2_ref, w_ref, g0_ref, g1_ref, g2_ref, o_ref):
    acc = None
    for n, (b_ref, g_ref) in enumerate(((b0_ref, g0_ref), (b1_ref, g1_ref), (b2_ref, g2_ref))):
        gate = 1.0 / (1.0 + jnp.exp(-g_ref[...]))
        t = gate * _mm(b_ref[...], w_ref[n])
        acc = t if acc is None else acc + t
    o_ref[...] = acc.astype(o_ref.dtype)


def merge(branches, w_branch, proj_g, *, tm, tn):
    M = proj_g.shape[0]
    nb = D_MODEL // tn
    bspec = pl.BlockSpec((tm, MIX_W), lambda i, j: (i, 0))
    gspec = lambda n: pl.BlockSpec((tm, tn), lambda i, j: (i, n * nb + j))
    return pl.pallas_call(
        _merge_kernel,
        grid=(M // tm, nb),
        in_specs=[bspec, bspec, bspec, pl.BlockSpec((3, MIX_W, tn), lambda i, j: (0, 0, j)),
                  gspec(0), gspec(1), gspec(2)],
        out_specs=pl.BlockSpec((tm, tn), lambda i, j: (i, j)),
        out_shape=jax.ShapeDtypeStruct((M, D_MODEL), bf16),
        compiler_params=_cp("parallel", "arbitrary"),
        name="merge",
    )(*branches, w_branch, proj_g, proj_g, proj_g)


def _memkv_kernel(kv_ref, g_ref, k_ref, v_ref):
    for h in range(MEM_HEADS):
        hs = slice(h * MEM_HEAD_DIM, (h + 1) * MEM_HEAD_DIM)
        k_ref[:, hs] = _rms(kv_ref[:, hs], g_ref[...])
    v_ref[...] = kv_ref[:, MEM_W:]


def memkv_post(kv, g):
    n = kv.shape[0]
    return pl.pallas_call(
        _memkv_kernel,
        grid=(1,),
        in_specs=[pl.BlockSpec((n, 2 * MEM_W), lambda i: (0, 0)), pl.BlockSpec((1, MEM_HEAD_DIM), lambda i: (0, 0))],
        out_specs=[pl.BlockSpec((n, MEM_W), lambda i: (0, 0))] * 2,
        out_shape=[jax.ShapeDtypeStruct((n, MEM_W), f32)] * 2,
        compiler_params=_cp("arbitrary"),
        name="memkv_post",
    )(kv, g)


def _memattn_p_kernel(q_ref, g_ref, k_ref, v_ref, o_ref):
    for h in range(MEM_HEADS):
        hs = slice(h * MEM_HEAD_DIM, (h + 1) * MEM_HEAD_DIM)
        q = _rms(q_ref[:, hs], g_ref[...]).astype(bf16)
        s = _nt(q, k_ref[:, hs].astype(bf16)) * (MEM_HEAD_DIM ** -0.5)
        p = jnp.exp(s - jnp.max(s, axis=1, keepdims=True))
        o = _mm(p.astype(bf16), v_ref[:, hs].astype(bf16)) / jnp.sum(p, axis=1, keepdims=True)
        o_ref[:, hs] = o.astype(o_ref.dtype)


def memattn_prompt(q, g, mem_k, mem_v, *, T, tm):
    n = mem_k.shape[0]
    return pl.pallas_call(
        _memattn_p_kernel,
        grid=(T // tm,),
        in_specs=[pl.BlockSpec((tm, MEM_W), lambda i: (i, 0)), pl.BlockSpec((1, MEM_HEAD_DIM), lambda i: (0, 0)),
                  pl.BlockSpec((n, MEM_W), lambda i: (0, 0)), pl.BlockSpec((n, MEM_W), lambda i: (0, 0))],
        out_specs=pl.BlockSpec((tm, MEM_W), lambda i: (i, 0)),
        out_shape=jax.ShapeDtypeStruct((T, MEM_W), bf16),
        compiler_params=_cp("parallel"),
        name="memattn_prompt",
    )(q, g, mem_k, mem_v)


def _memattn_s_kernel(q_ref, g_ref, k_ref, v_ref, o_ref):
    for h in range(MEM_HEADS):
        hs = slice(h * MEM_HEAD_DIM, (h + 1) * MEM_HEAD_DIM)
        q = _rms(q_ref[:, :, hs], g_ref[...])
        s = jnp.sum(k_ref[:, :, h, :] * q, axis=-1, keepdims=True) * (MEM_HEAD_DIM ** -0.5)
        p = jnp.exp(s - jnp.max(s, axis=1, keepdims=True))
        o = jnp.sum(p * v_ref[:, :, h, :], axis=1, keepdims=True) / jnp.sum(p, axis=1, keepdims=True)
        o_ref[:, :, hs] = o.astype(o_ref.dtype)


def memattn_sample(q, g, cache_k, cache_v, *, layer, T, B, tb):
    n = cache_k.shape[2]
    q3 = q.reshape(q.shape[0], 1, MEM_W)
    ck, cv = cache_k, cache_v
    kvspec = pl.BlockSpec((None, tb, n, MEM_HEADS, MEM_HEAD_DIM), lambda i: (layer, i, 0, 0, 0))
    out = pl.pallas_call(
        _memattn_s_kernel,
        grid=(B // tb,),
        in_specs=[pl.BlockSpec((tb, 1, MEM_W), lambda i: (T // tb + i, 0, 0)),
                  pl.BlockSpec((1, MEM_HEAD_DIM), lambda i: (0, 0)), kvspec, kvspec],
        out_specs=pl.BlockSpec((tb, 1, MEM_W), lambda i: (i, 0, 0)),
        out_shape=jax.ShapeDtypeStruct((B, 1, MEM_W), bf16),
        compiler_params=_cp("parallel"),
        name="memattn_sample",
    )(q3, g, ck, cv)
    return out.reshape(B, MEM_W)


def _ffn1_kernel(x_ref, g_ref, wg_ref, wu_ref, *rest, scaled):
    if scaled:
        c_ref, o_ref, xn_ref = rest
    else:
        o_ref, xn_ref = rest

    @pl.when(pl.program_id(1) == 0)
    def _():
        xn_ref[...] = _rms(x_ref[...], g_ref[...]).astype(bf16)

    xn = xn_ref[...]
    a = _mm(xn, wg_ref[...])
    a = a / (1.0 + jnp.exp(-a)) * _mm(xn, wu_ref[...])
    if scaled:
        a = a * c_ref[:, :1]
    o_ref[...] = a.astype(o_ref.dtype)


def ffn1(x, g, wg, wu, *, tm, tn, comb=None):
    M, K = x.shape
    F = wg.shape[0] * wg.shape[2]
    nper = wg.shape[2] // tn
    wspec = pl.BlockSpec((None, K, tn), lambda i, j: (j // nper, 0, j % nper))
    in_specs = [pl.BlockSpec((tm, K), lambda i, j: (i, 0)), pl.BlockSpec((1, K), lambda i, j: (0, 0)), wspec, wspec]
    args = [x, g, wg, wu]
    if comb is not None:
        in_specs.append(pl.BlockSpec((tm, 128), lambda i, j: (i, j // nper)))
        args.append(comb)
    return pl.pallas_call(
        functools.partial(_ffn1_kernel, scaled=comb is not None),
        grid=(M // tm, F // tn),
        in_specs=in_specs,
        out_specs=pl.BlockSpec((tm, tn), lambda i, j: (i, j)),
        out_shape=jax.ShapeDtypeStruct((M, F), bf16),
        scratch_shapes=[pltpu.VMEM((tm, K), bf16)],
        compiler_params=_cp("parallel", "arbitrary"),
        name="ffn1",
    )(*args)


def _router_kernel(x_ref, g_ref, w_ref, b_ref, e_ref, o_ref):
    logits = _mm(_rms(x_ref[...], g_ref[...]), w_ref[...], precision=HI) + b_ref[...]
    lane = lax.broadcasted_iota(jnp.int32, logits.shape, 1)
    logits = jnp.where(lane < N_EXPERTS, logits, -jnp.inf)
    m1 = jnp.max(logits, axis=1, keepdims=True)
    i1 = jnp.min(jnp.where(logits == m1, lane, 128), axis=1, keepdims=True)
    rest = jnp.where(lane == i1, -jnp.inf, logits)
    m2 = jnp.max(rest, axis=1, keepdims=True)
    i2 = jnp.min(jnp.where(rest == m2, lane, 128), axis=1, keepdims=True)
    e2 = jnp.exp(m2 - m1)
    w1 = 1.0 / (1.0 + e2)
    w2 = e2 / (1.0 + e2)
    comb = jnp.where(lane == i1, w1, 0.0) + jnp.where(lane == i2, w2, 0.0)
    o_ref[...] = _mm(comb, e_ref[...], precision=HI)


def router(x, g, w_pad, b_pad, expand, *, tm):
    M, K = x.shape
    return pl.pallas_call(
        _router_kernel,
        grid=(M // tm,),
        in_specs=[pl.BlockSpec((tm, K), lambda i: (i, 0)), pl.BlockSpec((1, K), lambda i: (0, 0)),
                  pl.BlockSpec((K, 128), lambda i: (0, 0)), pl.BlockSpec((1, 128), lambda i: (0, 0)),
                  pl.BlockSpec((128, N_EXPERTS * 128), lambda i: (0, 0))],
        out_specs=pl.BlockSpec((tm, N_EXPERTS * 128), lambda i: (i, 0)),
        out_shape=jax.ShapeDtypeStruct((M, N_EXPERTS * 128), f32),
        compiler_params=_cp("parallel"),
        name="router",
    )(x, g, w_pad, b_pad, expand)


def _swap_halves(x):
    h = x.shape[-1] // 2
    return jnp.concatenate([x[..., h:], x[..., :h]], axis=-1)


def kernel(x_prompt, x_sample, mem_prompt, state_pool, cache_mla_latent, cache_mla_rope, state_rwkv, state_rwkv_shift, cache_mem_k, cache_mem_v, page_table, norm_mix_g, w_in, pool_w, pool_scale, mla_cq_g, mla_w_uq, mla_ckv_g, mla_kr_g, mla_w_uk, mla_w_uv, mla_qn_g, mla_qr_g, mla_kn_g, rwkv_mu, rwkv_w0, rwkv_w2, rwkv_a0, rwkv_a2, rwkv_g2, rwkv_k_k, rwkv_k_a, rwkv_r_k, rwkv_ln_w, rwkv_ln_b, w_branch, w_out, norm_mem_g, mem_norm_g, w_q_mem, w_k_mem, w_v_mem, mem_qn_g, mem_kn_g, w_o_mem, norm_ffn_g, ffn_w_gate, ffn_w_up, ffn_w_down, moe_router, moe_router_b, moe_w_gate, moe_w_up, moe_w_down):
    depth = w_in.shape[0]
    T = x_prompt.shape[1]
    B = x_sample.shape[0]
    assert x_prompt.shape[0] == 1 and x_sample.shape[1] == 1
    n_pages = page_table.shape[1]
    past_len = n_pages * PAGE_SIZE
    M = T + B
    assert T % B == 0 and T % SCAN_CHUNK == 0 and B % 8 == 0

    tm = _tile(M, 640)
    tp = _tile(T, 512)
    tq = _tile(T, 512, 128)
    tb = _tile(B, 8)
    G = _tile(n_pages, 32, 2)
    row = lambda v: v.reshape(1, -1)

    x = jnp.concatenate([x_prompt[0], x_sample[:, 0]], axis=0)
    cache_rope_t = jnp.swapaxes(cache_mla_rope, 2, 3)

    pos = jnp.concatenate([jnp.arange(T, dtype=jnp.int32), jnp.full((B,), past_len, jnp.int32)])
    inv = ROPE_THETA ** (-jnp.arange(0, MLA_ROPE, 2, dtype=f32) / MLA_ROPE)
    ang = pos.astype(f32)[:, None] * inv[None, :]
    cos, sin = jnp.cos(ang), jnp.sin(ang)
    tab = jnp.concatenate([cos, cos, -sin, sin], axis=1)

    ids = jnp.arange(MIX_W) // RWKV_HEAD
    ind = (ids[:, None] == ids[None, :]).astype(bf16)
    expand = (jnp.arange(128)[:, None] == (jnp.arange(N_EXPERTS * 128) // 128)[None, :]).astype(f32)

    outs = {k: [] for k in ("pool_p", "lat_p", "rope_p", "rwkv_p", "shift_p", "memk_p", "memv_p",
                            "pool_s", "lat_s", "rope_s", "rwkv_s", "shift_s")}
    for l in range(depth):
        wi = w_in[l]
        kr_cols = wi[:, OFF_KR:OFF_RWKV]
        w_a = jnp.concatenate([wi[:, :OFF_KR], kr_cols, _swap_halves(kr_cols)], axis=1).astype(bf16)
        w_z = wi[:, OFF_RWKV:OFF_GATE].astype(bf16)
        w_g = wi[:, OFF_GATE:].astype(bf16)
        uq = mla_w_uq[l].reshape(MLA_Q_LORA, MLA_HEADS, MLA_QK)
        w_big = jnp.concatenate([uq, _swap_halves(uq[..., MLA_NOPE:])], axis=-1)
        w_big = w_big.reshape(MLA_Q_LORA, MLA_HEADS * HEAD_PAD).astype(bf16)
        gq = row(jnp.concatenate([mla_qn_g[l], mla_qr_g[l], _swap_halves(mla_qr_g[l])]) * MLA_SCALE)
        gk = row(jnp.concatenate([mla_kr_g[l], _swap_halves(mla_kr_g[l])]))
        w_uk = mla_w_uk[l].reshape(MLA_KV_LORA, MLA_HEADS * MLA_NOPE)
        w_uk_b = w_uk.astype(bf16)
        wuk_t = w_uk.T.astype(bf16)
        w_abs = (mla_w_uk[l] * mla_kn_g[l][None, None, :]).transpose(1, 2, 0)
        w_abs = jnp.concatenate([w_abs, jnp.zeros((MLA_HEADS, HEAD_PAD - MLA_NOPE, MLA_KV_LORA), f32)], axis=1).astype(bf16)
        w_uv = mla_w_uv[l].transpose(1, 0, 2).astype(bf16)
        wcat = jnp.zeros((256, 3 * MIX_W), f32)
        wcat = wcat.at[:64, :MIX_W].set(rwkv_w2[l]).at[64:128, MIX_W:2 * MIX_W].set(rwkv_a2[l])
        wcat = wcat.at[128:, 2 * MIX_W:].set(rwkv_g2[l]).astype(bf16)
        rparams = [row(rwkv_mu[l]), wcat, row(rwkv_w0[l]), row(rwkv_a0[l]), row(rwkv_k_k[l]), row(rwkv_k_a[l]),
                   row(rwkv_r_k[l]), ind]

        g_mix = row(norm_mix_g[l])
        proj_a = fmm_norm(x, g_mix, w_a, tm=tm, tn=w_a.shape[1])
        proj_z = fmm_norm(x, g_mix, w_z, tm=tm, tn=RWKV_PROJ // 2)
        proj_g = fmm_norm(x, g_mix, w_g, tm=tm, tn=1536)

        pw = pool_w[l].astype(bf16)
        o_pool = jnp.concatenate([
            pool_prompt(proj_a, pw, row(pool_scale[l]), T=T, tm=tp),
            pool_sample(state_pool[l].transpose(1, 0, 2), proj_a, pw, row(pool_scale[l]), T=T, B=B, past_len=past_len)])
        u = proj_a[:, :MIX_W]
        outs["pool_p"].append(u[T - POOL_HIST:T][None])
        outs["pool_s"].append(jnp.concatenate([state_pool[l][:, 1:], u[T:, None]], axis=1))

        q_full = qprep(proj_a, row(mla_cq_g[l]), w_big, gq, tab, tm=tm)
        lat, lat_b, krope, k_full = kprep(proj_a, row(mla_ckv_g[l]), gk, tab, w_uk_b, row(mla_kn_g[l]), tm=tm)
        outs["lat_p"].append(lat[:T][None])
        outs["lat_s"].append(lat[T:, None])
        outs["rope_p"].append(krope[:T, :MLA_ROPE][None])
        outs["rope_s"].append(krope[T:, None, :MLA_ROPE])
        o_lat_p = attn_prompt(q_full, k_full, lat_b, T=T, tq=tq)
        q_s = q_full[T:]
        qa = gmm(q_s, w_abs, tm=B, out_dtype=bf16).reshape(B, MLA_HEADS, MLA_KV_LORA)
        q_s3 = q_s.reshape(B, MLA_HEADS, HEAD_PAD)
        o_lat_s = attn_sample(page_table, qa, q_s3[:, :, MLA_NOPE:MLA_QK], q_s3,
                              k_full[T:].reshape(B, MLA_HEADS, HEAD_PAD), lat[T:, None], wuk_t,
                              cache_mla_latent, cache_rope_t, layer=l, G=G)
        o_lat = jnp.concatenate([o_lat_p, o_lat_s.reshape(B, MLA_HEADS * MLA_KV_LORA)])
        o_mla = gmm(o_lat, w_uv, tm=tm, out_dtype=bf16)

        (rp, rs_) = rwkv_prep(proj_z, state_rwkv_shift[l], rparams, T=T, B=B, tm=_tile(T, 256))
        y_p, st_t = rwkv_scan(rp[0], rp[1], rp[2], rp[3], rp[4], rp[5], exact=SCAN_EXACT)
        s_new, y_s = rwkv_step(state_rwkv[l], rs_[0], rs_[1], rs_[2], rs_[3], rs_[4], rs_[5])
        outs["rwkv_p"].append(st_t.transpose(0, 2, 1)[None])
        outs["rwkv_s"].append(s_new)
        outs["shift_p"].append(proj_z[T - 1:T])
        outs["shift_s"].append(proj_z[T:])
        cat = lambda a, b: jnp.concatenate([a, b])
        o_rwkv = rwkv_post(cat(y_p, y_s), cat(rp[7], rs_[7]), cat(rp[6], rs_[6]),
                           row(rwkv_ln_w[l]), row(rwkv_ln_b[l]), ind, tm=tm)

        merged = merge((o_pool, o_mla, o_rwkv), w_branch[l].astype(bf16), proj_g, tm=tm, tn=512)
        x = fmm(merged, w_out[l].astype(bf16), tm=tm, tn=1024, tk=D_MODEL, res=x)

        w_kv = jnp.concatenate([w_k_mem[l], w_v_mem[l]], axis=1).astype(bf16)
        n_mem = mem_prompt.shape[1]
        kv = fmm_norm(mem_prompt[0], row(mem_norm_g[l]), w_kv, tm=n_mem, tn=2 * MEM_W)
        mem_k, mem_v = memkv_post(kv, row(mem_kn_g[l]))
        outs["memk_p"].append(mem_k.reshape(1, n_mem, MEM_HEADS, MEM_HEAD_DIM))
        outs["memv_p"].append(mem_v.reshape(1, n_mem, MEM_HEADS, MEM_HEAD_DIM))
        q_mem = fmm_norm(x, row(norm_mem_g[l]), w_q_mem[l].astype(bf16), tm=tm, tn=MEM_W)
        o_mem = jnp.concatenate([
            memattn_prompt(q_mem, row(mem_qn_g[l]), mem_k, mem_v, T=T, tm=tp),
            memattn_sample(q_mem, row(mem_qn_g[l]), cache_mem_k, cache_mem_v, layer=l, T=T, B=B, tb=tb)])
        x = fmm(o_mem, w_o_mem[l].astype(bf16), tm=tm, tn=1024, tk=MEM_W, res=x)

        g_ffn = row(norm_ffn_g[l])
        if l % 2 == 0:
            d = l // 2
            a = ffn1(x, g_ffn, ffn_w_gate[d].astype(bf16)[None], ffn_w_up[d].astype(bf16)[None], tm=tm, tn=1408)
            x = fmm(a, ffn_w_down[d].astype(bf16), tm=tm, tn=1024, tk=1408, res=x)
        else:
            e = l // 2
            w_r = jnp.pad(moe_router[e], ((0, 0), (0, 128 - N_EXPERTS)))
            b_r = row(jnp.pad(moe_router_b[e], (0, 128 - N_EXPERTS)))
            comb = router(x, g_ffn, w_r, b_r, expand, tm=tm)
            fe = moe_w_gate.shape[-1]
            wd = moe_w_down[e].reshape(N_EXPERTS * fe, D_MODEL).astype(bf16)
            a = ffn1(x, g_ffn, moe_w_gate[e].astype(bf16), moe_w_up[e].astype(bf16), tm=tm, tn=1408, comb=comb)
            x = fmm(a, wd, tm=tm, tn=1024, tk=fe, res=x)

    st = lambda k: jnp.stack(outs[k])
    return (x[:T][None], x[T:, None],
            st("pool_p"), st("lat_p"), st("rope_p"), st("rwkv_p"), st("shift_p"), st("memk_p"), st("memv_p"),
            st("pool_s"), st("lat_s"), st("rope_s"), st("rwkv_s"), st("shift_s"))
```

```python
import functools

import jax
import jax.numpy as jnp
from jax import lax
from jax.experimental import pallas as pl
from jax.experimental.pallas import tpu as pltpu

f32 = jnp.float32
bf16 = jnp.bfloat16

D_MODEL = 2048
MIX_W = 1024
POOL_WINDOWS = (2, 4, 8, 16)
POOL_GW = 256
POOL_HIST = 15
MLA_HEADS = 8
MLA_NOPE = 128
MLA_ROPE = 64
MLA_QK = MLA_NOPE + MLA_ROPE
MLA_Q_LORA = 512
MLA_KV_LORA = 256
MLA_SCALE = MLA_QK ** -0.5
ROPE_THETA = 10000.0
PAGE_SIZE = 128
RWKV_HEAD = 64
RWKV_HEADS = 16
RWKV_PROJ = 3328
MEM_HEADS = 4
MEM_HEAD_DIM = 128
MEM_W = 512
N_EXPERTS = 8
RMS_EPS = 1e-6
GN_EPS = 64e-5
NEG_INF = -1e30
OFF_CQ, OFF_CKV, OFF_KR, OFF_RWKV, OFF_GATE = 1024, 1536, 1792, 1856, 5184
HEAD_PAD = 256
PAGE_SLOTS = 3
HEAD_GROUP = 4
SCAN_CHUNK = 64
SCAN_CHUNKS_PER_STEP = 4
VMEM_LIMIT = 50 * 1024 * 1024
HI = lax.Precision.HIGHEST
SCAN_EXACT = False


def _cp(*sem, vmem=VMEM_LIMIT):
    return pltpu.CompilerParams(dimension_semantics=sem, vmem_limit_bytes=vmem)


def _tile(n, pref, mult=8):
    best = None
    t = mult
    while t <= min(n, pref):
        if n % t == 0:
            best = t
        t += mult
    return best if best is not None else n


def _nt(a, b, precision=None):
    return lax.dot_general(a, b, (((1,), (1,)), ((), ())), precision=precision,
                           preferred_element_type=f32)


def _mm(a, b, precision=None):
    return jnp.dot(a, b, precision=precision, preferred_element_type=f32)


def _rms(x, g):
    return x * lax.rsqrt(jnp.mean(x * x, axis=-1, keepdims=True) + RMS_EPS) * g


def _segsum(x, ind):
    hi = x.astype(bf16)
    lo = (x - hi.astype(f32)).astype(bf16)
    return _mm(hi, ind) + _mm(lo, ind)


def _fmm_norm_kernel(x_ref, g_ref, w_ref, *rest, has_res):
    if has_res:
        r_ref, o_ref, xn_ref = rest
    else:
        o_ref, xn_ref = rest

    @pl.when(pl.program_id(1) == 0)
    def _():
        xn_ref[...] = _rms(x_ref[...], g_ref[...]).astype(bf16)

    acc = _mm(xn_ref[...], w_ref[...])
    if has_res:
        acc = acc + r_ref[...]
    o_ref[...] = acc.astype(o_ref.dtype)


def fmm_norm(x, g, w, *, tm, tn, out_dtype=f32, res=None):
    M, K = x.shape
    N = w.shape[1]
    in_specs = [pl.BlockSpec((tm, K), lambda i, j: (i, 0)),
                pl.BlockSpec((1, K), lambda i, j: (0, 0)),
                pl.BlockSpec((K, tn), lambda i, j: (0, j))]
    args = [x, g, w]
    if res is not None:
        in_specs.append(pl.BlockSpec((tm, tn), lambda i, j: (i, j)))
        args.append(res)
    return pl.pallas_call(
        functools.partial(_fmm_norm_kernel, has_res=res is not None),
        grid=(M // tm, N // tn),
        in_specs=in_specs,
        out_specs=pl.BlockSpec((tm, tn), lambda i, j: (i, j)),
        out_shape=jax.ShapeDtypeStruct((M, N), out_dtype),
        scratch_shapes=[pltpu.VMEM((tm, K), bf16)],
        compiler_params=_cp("parallel", "arbitrary"),
        name="fmm_norm",
    )(*args)


def _fmm_kernel(x_ref, w_ref, *rest, has_res, nk):
    if has_res:
        r_ref, o_ref, acc_ref = rest
    else:
        o_ref, acc_ref = rest
    k = pl.program_id(2)

    @pl.when(k == 0)
    def _():
        acc_ref[...] = jnp.zeros_like(acc_ref)

    acc_ref[...] += _mm(x_ref[...].astype(bf16), w_ref[...])

    @pl.when(k == nk - 1)
    def _():
        a = acc_ref[...]
        if has_res:
            a = a + r_ref[...]
        o_ref[...] = a.astype(o_ref.dtype)


def fmm(x, w, *, tm, tn, tk, out_dtype=f32, res=None):
    M, K = x.shape
    N = w.shape[1]
    nk = K // tk
    in_specs = [pl.BlockSpec((tm, tk), lambda i, j, k: (i, k)),
                pl.BlockSpec((tk, tn), lambda i, j, k: (k, j))]
    args = [x, w]
    if res is not None:
        in_specs.append(pl.BlockSpec((tm, tn), lambda i, j, k: (i, j)))
        args.append(res)
    return pl.pallas_call(
        functools.partial(_fmm_kernel, has_res=res is not None, nk=nk),
        grid=(M // tm, N // tn, nk),
        in_specs=in_specs,
        out_specs=pl.BlockSpec((tm, tn), lambda i, j, k: (i, j)),
        out_shape=jax.ShapeDtypeStruct((M, N), out_dtype),
        scratch_shapes=[pltpu.VMEM((tm, tn), f32)],
        compiler_params=_cp("parallel", "parallel", "arbitrary"),
        name="fmm",
    )(*args)


def _gmm_kernel(x_ref, w_ref, o_ref):
    o_ref[...] = _mm(x_ref[...].astype(bf16), w_ref[...]).astype(o_ref.dtype)


def gmm(x, w, *, tm, out_dtype=f32):
    M = x.shape[0]
    G, K, N = w.shape
    return pl.pallas_call(
        _gmm_kernel,
        grid=(M // tm, G),
        in_specs=[pl.BlockSpec((tm, K), lambda i, g: (i, g)),
                  pl.BlockSpec((None, K, N), lambda i, g: (g, 0, 0))],
        out_specs=pl.BlockSpec((tm, N), lambda i, g: (i, g)),
        out_shape=jax.ShapeDtypeStruct((M, G * N), out_dtype),
        compiler_params=_cp("parallel", "arbitrary"),
        name="gmm",
    )(x, w)


def _pool_p_kernel(u_ref, halo_ref, w_ref, sc_ref, o_ref, ext_ref, *, tm):
    i = pl.program_id(0)
    ext_ref[0:16, :] = jnp.where(i > 0, halo_ref[...], 0.0)
    ext_ref[16:, :] = u_ref[...]
    pos = i * tm + lax.broadcasted_iota(jnp.int32, (tm, 1), 0)
    for g, w in enumerate(POOL_WINDOWS):
        cols = slice(g * POOL_GW, (g + 1) * POOL_GW)
        s = ext_ref[16:, cols]
        for d in range(1, w):
            s = s + ext_ref[pl.ds(16 - d, tm), cols]
        pooled = s / jnp.minimum(pos + 1, w).astype(f32)
        diff = (pooled - ext_ref[16:, cols]).astype(bf16)
        o_ref[:, cols] = (_mm(diff, w_ref[g]) * sc_ref[:, cols]).astype(o_ref.dtype)


def pool_prompt(proj_a, pool_w, scale, *, T, tm):
    return pl.pallas_call(
        functools.partial(_pool_p_kernel, tm=tm),
        grid=(T // tm,),
        in_specs=[pl.BlockSpec((tm, MIX_W), lambda i: (i, 0)),
                  pl.BlockSpec((16, MIX_W), lambda i: (jnp.maximum(i * (tm // 16) - 1, 0), 0)),
                  pl.BlockSpec((4, POOL_GW, POOL_GW), lambda i: (0, 0, 0)),
                  pl.BlockSpec((1, MIX_W), lambda i: (0, 0))],
        out_specs=pl.BlockSpec((tm, MIX_W), lambda i: (i, 0)),
        out_shape=jax.ShapeDtypeStruct((T, MIX_W), bf16),
        scratch_shapes=[pltpu.VMEM((tm + 16, MIX_W), f32)],
        compiler_params=_cp("parallel"),
        name="pool_prompt",
    )(proj_a, proj_a, pool_w, scale)


def _pool_s_kernel(h_ref, u_ref, w_ref, sc_ref, o_ref, *, counts):
    for g, w in enumerate(POOL_WINDOWS):
        cols = slice(g * POOL_GW, (g + 1) * POOL_GW)
        u = u_ref[:, cols]
        s = u
        for d in range(1, w):
            s = s + h_ref[POOL_HIST - d, :, cols]
        diff = (s / counts[g] - u).astype(bf16)
        o_ref[:, cols] = (_mm(diff, w_ref[g]) * sc_ref[:, cols]).astype(o_ref.dtype)


def pool_sample(hist_t, proj_a, pool_w, scale, *, T, B, past_len):
    counts = tuple(float(min(past_len + 1, w)) for w in POOL_WINDOWS)
    return pl.pallas_call(
        functools.partial(_pool_s_kernel, counts=counts),
        grid=(1,),
        in_specs=[pl.BlockSpec((POOL_HIST, B, MIX_W), lambda i: (0, 0, 0)),
                  pl.BlockSpec((B, MIX_W), lambda i: (T // B, 0)),
                  pl.BlockSpec((4, POOL_GW, POOL_GW), lambda i: (0, 0, 0)),
                  pl.BlockSpec((1, MIX_W), lambda i: (0, 0))],
        out_specs=pl.BlockSpec((B, MIX_W), lambda i: (0, 0)),
        out_shape=jax.ShapeDtypeStruct((B, MIX_W), bf16),
        compiler_params=_cp("arbitrary"),
        name="pool_sample",
    )(hist_t, proj_a, pool_w, scale)


def _rope_half(hi, coef):
    rs = lax.rsqrt(jnp.sum(hi * hi, axis=-1, keepdims=True) * (0.5 / MLA_ROPE) + RMS_EPS)
    t = hi * rs * coef
    r = t + pltpu.roll(t, 64, 1)
    lane = lax.broadcasted_iota(jnp.int32, (1, 128), 1)
    return jnp.where(lane < MLA_ROPE, r, 0.0)


def _qprep_kernel(c_ref, g_ref, w_ref, gq_ref, tab_ref, o_ref):
    cn = _rms(c_ref[...], g_ref[...]).astype(bf16)
    x = _mm(cn, w_ref[...])
    gq = gq_ref[...]
    coef = tab_ref[...] * gq[:, MLA_NOPE:]
    for h in range(MLA_HEADS):
        b = h * HEAD_PAD
        o_ref[:, b:b + MLA_NOPE] = _rms(x[:, b:b + MLA_NOPE], gq[:, :MLA_NOPE]).astype(o_ref.dtype)
        o_ref[:, b + MLA_NOPE:b + HEAD_PAD] = _rope_half(x[:, b + MLA_NOPE:b + HEAD_PAD], coef).astype(o_ref.dtype)


def qprep(proj_a, cq_g, w_big, gq, tab, *, tm):
    M = proj_a.shape[0]
    return pl.pallas_call(
        _qprep_kernel,
        grid=(M // tm,),
        in_specs=[pl.BlockSpec((tm, MLA_Q_LORA), lambda i: (i, OFF_CQ // MLA_Q_LORA)),
                  pl.BlockSpec((1, MLA_Q_LORA), lambda i: (0, 0)),
                  pl.BlockSpec((MLA_Q_LORA, MLA_HEADS * HEAD_PAD), lambda i: (0, 0)),
                  pl.BlockSpec((1, HEAD_PAD), lambda i: (0, 0)),
                  pl.BlockSpec((tm, 128), lambda i: (i, 0))],
        out_specs=pl.BlockSpec((tm, MLA_HEADS * HEAD_PAD), lambda i: (i, 0)),
        out_shape=jax.ShapeDtypeStruct((M, MLA_HEADS * HEAD_PAD), bf16),
        compiler_params=_cp("parallel"),
        name="qprep",
    )(proj_a, cq_g, w_big, gq, tab)


def _kprep_kernel(c_ref, g_ref, gk_ref, tab_ref, w_ref, kng_ref, lat_ref, latb_ref, kr_ref, k_ref):
    ck = c_ref[...]
    lat = _rms(ck[:, :MLA_KV_LORA], g_ref[...])
    lat_ref[...] = lat
    latb = lat.astype(bf16)
    latb_ref[...] = latb
    kr = _rope_half(ck[:, MLA_KV_LORA:], tab_ref[...] * gk_ref[...])
    kr_ref[...] = kr
    krb = kr.astype(bf16)
    kn = _mm(latb, w_ref[...])
    for h in range(MLA_HEADS):
        b = h * HEAD_PAD
        k_ref[:, b:b + MLA_NOPE] = _rms(kn[:, h * MLA_NOPE:(h + 1) * MLA_NOPE], kng_ref[...]).astype(bf16)
        k_ref[:, b + MLA_NOPE:b + HEAD_PAD] = krb


def kprep(proj_a, ckv_g, gk, tab, w_uk, kn_g, *, tm):
    M = proj_a.shape[0]
    wk = MLA_KV_LORA + 128
    row = lambda n: pl.BlockSpec((tm, n), lambda i: (i, 0))
    return pl.pallas_call(
        _kprep_kernel,
        grid=(M // tm,),
        in_specs=[pl.BlockSpec((tm, wk), lambda i: (i, OFF_CKV // wk)),
                  pl.BlockSpec((1, MLA_KV_LORA), lambda i: (0, 0)),
                  pl.BlockSpec((1, 128), lambda i: (0, 0)),
                  row(128),
                  pl.BlockSpec((MLA_KV_LORA, MLA_HEADS * MLA_NOPE), lambda i: (0, 0)),
                  pl.BlockSpec((1, MLA_NOPE), lambda i: (0, 0))],
        out_specs=[row(MLA_KV_LORA), row(MLA_KV_LORA), row(128), row(MLA_HEADS * HEAD_PAD)],
        out_shape=[jax.ShapeDtypeStruct((M, MLA_KV_LORA), f32),
                   jax.ShapeDtypeStruct((M, MLA_KV_LORA), bf16),
                   jax.ShapeDtypeStruct((M, 128), f32),
                   jax.ShapeDtypeStruct((M, MLA_HEADS * HEAD_PAD), bf16)],
        compiler_params=_cp("parallel"),
        name="kprep",
    )(proj_a, ckv_g, gk, tab, w_uk, kn_g)


def _attn_p_kernel(qi_ref, ki_ref, q_ref, k_ref, v_ref, o_ref, m_ref, l_ref, acc_ref, *, tq):
    qi = qi_ref[pl.program_id(0)]
    ki = ki_ref[pl.program_id(0)]

    @pl.when(ki == 0)
    def _():
        m_ref[...] = jnp.full_like(m_ref, NEG_INF)
        l_ref[...] = jnp.zeros_like(l_ref)
        acc_ref[...] = jnp.zeros_like(acc_ref)

    def step(masked):
        if masked:
            rows = lax.broadcasted_iota(jnp.int32, (tq, tq), 0)
            cols = lax.broadcasted_iota(jnp.int32, (tq, tq), 1)
            bias = jnp.where(cols <= rows, 0.0, NEG_INF)
        v = v_ref[...]
        rep = tq // 128
        for h0 in range(0, MLA_HEADS, HEAD_GROUP):
            H = range(h0, h0 + HEAD_GROUP)
            hs = {h: slice(h * HEAD_PAD, (h + 1) * HEAD_PAD) for h in H}
            s = {h: _nt(q_ref[:, hs[h]], k_ref[:, hs[h]]) for h in H}
            if masked:
                s = {h: s[h] + bias for h in H}
            m_prev = {h: m_ref[h] for h in H}
            m_new = {h: jnp.maximum(m_prev[h], jnp.max(s[h], axis=1, keepdims=True)) for h in H}
            p = {h: jnp.exp(s[h] - jnp.concatenate([m_new[h]] * rep, axis=1)) for h in H}
            alpha = {h: jnp.exp(m_prev[h] - m_new[h]) for h in H}
            pv = {h: _mm(p[h].astype(bf16), v) for h in H}
            for h in H:
                l_ref[h] = alpha[h] * l_ref[h] + jnp.sum(p[h], axis=1, keepdims=True)
                acc_ref[h] = jnp.concatenate([alpha[h]] * (MLA_KV_LORA // 128), axis=1) * acc_ref[h] + pv[h]
                m_ref[h] = m_new[h]

    @pl.when(ki < qi)
    def _():
        step(False)

    @pl.when(ki == qi)
    def _():
        step(True)

    @pl.when(ki == qi)
    def _():
        for h in range(MLA_HEADS):
            hs = slice(h * HEAD_PAD, (h + 1) * HEAD_PAD)
            o_ref[:, hs] = (acc_ref[h] / l_ref[h][:, :1]).astype(o_ref.dtype)


def attn_prompt(q_full, k_full, lat_b, *, T, tq):
    n = T // tq
    W = MLA_HEADS * HEAD_PAD
    pairs = [(qi, ki) for qi in range(n) for ki in range(qi + 1)]
    qi_tab = jnp.asarray([p[0] for p in pairs], jnp.int32)
    ki_tab = jnp.asarray([p[1] for p in pairs], jnp.int32)
    return pl.pallas_call(
        functools.partial(_attn_p_kernel, tq=tq),
        grid_spec=pltpu.PrefetchScalarGridSpec(
            num_scalar_prefetch=2,
            grid=(len(pairs),),
            in_specs=[pl.BlockSpec((tq, W), lambda i, qt, kt: (qt[i], 0)),
                      pl.BlockSpec((tq, W), lambda i, qt, kt: (kt[i], 0)),
                      pl.BlockSpec((tq, MLA_KV_LORA), lambda i, qt, kt: (kt[i], 0))],
            out_specs=pl.BlockSpec((tq, W), lambda i, qt, kt: (qt[i], 0)),
            scratch_shapes=[pltpu.VMEM((MLA_HEADS, tq, 128), f32),
                            pltpu.VMEM((MLA_HEADS, tq, 128), f32),
                            pltpu.VMEM((MLA_HEADS, tq, MLA_KV_LORA), f32)]),
        out_shape=jax.ShapeDtypeStruct((T, W), bf16),
        compiler_params=_cp("arbitrary"),
        name="attn_prompt",
    )(qi_tab, ki_tab, q_full, k_full, lat_b)


def _attn_s_kernel(pt_ref, qa_ref, qr_ref, qf_ref, kn_ref, ln_ref, wt_ref, lat_hbm, rope_hbm, o_ref,
                   lat_buf, rope_buf, sem, m_ref, l_ref, acc_ref, *, G, ng, layer, nsteps):
    b = pl.program_id(0)
    ahead = PAGE_SLOTS - 1

    def page_copies(step, slot):
        out = []
        for i in range(G):
            page = pt_ref[step * G + i]
            out.append(pltpu.make_async_copy(lat_hbm.at[layer, page], lat_buf.at[slot, i], sem.at[slot]))
            out.append(pltpu.make_async_copy(rope_hbm.at[layer, page], rope_buf.at[slot, i], sem.at[slot]))
        return out

    @pl.when(b == 0)
    def _():
        for k in range(ahead):
            for c in page_copies(jnp.minimum(k, nsteps - 1), k):
                c.start()

    m_ref[...] = jnp.full_like(m_ref, NEG_INF)
    l_ref[...] = jnp.zeros_like(l_ref)
    acc_ref[...] = jnp.zeros_like(acc_ref)

    def step(g, carry):
        s = b * ng + g
        slot = lax.rem(s, PAGE_SLOTS)
        for c in page_copies(s, slot):
            c.wait()
        qa = qa_ref[...]
        qr = qr_ref[...]
        wt = wt_ref[...]
        lat_all = jnp.concatenate([lat_buf[slot, i].astype(bf16) for i in range(G)], axis=0)
        pairs = range(0, G, 2)
        lat = [lat_all[i * PAGE_SIZE:(i + 2) * PAGE_SIZE] for i in pairs]
        rope_t = [jnp.concatenate([rope_buf[slot, i], rope_buf[slot, i + 1]], axis=1).astype(bf16) for i in pairs]
        kt = [_nt(wt, x) for x in lat]
        num = [_nt(qa, x) for x in lat]
        rr = [_mm(qr, x) for x in rope_t]
        ssq = [jnp.sum((x * x).reshape(MLA_HEADS, MLA_NOPE, 2 * PAGE_SIZE), axis=1) for x in kt]
        parts = [n * lax.rsqrt(q * (1.0 / MLA_NOPE) + RMS_EPS) + r for n, q, r in zip(num, ssq, rr)]
        sc = jnp.concatenate(parts, axis=1)
        m_prev = m_ref[...]
        m_new = jnp.maximum(m_prev, jnp.max(sc, axis=1, keepdims=True))
        alpha = jnp.exp(m_prev - m_new)
        p = jnp.exp(sc - m_new[:, :1])
        l_ref[...] = alpha * l_ref[...] + jnp.sum(p, axis=1, keepdims=True)
        acc_ref[...] = alpha[:, :1] * acc_ref[...] + _mm(p.astype(bf16), lat_all)
        m_ref[...] = m_new
        for c in page_copies(jnp.minimum(s + ahead, nsteps - 1), lax.rem(s + ahead, PAGE_SLOTS)):
            c.start()
        return carry

    lax.fori_loop(0, ng, step, 0)

    s_new = jnp.sum(qf_ref[...].astype(f32) * kn_ref[...].astype(f32), axis=1, keepdims=True)
    m_prev = m_ref[...]
    m_new = jnp.maximum(m_prev, s_new)
    alpha = jnp.exp(m_prev - m_new)
    p = jnp.exp(s_new - m_new)
    l = alpha * l_ref[...] + p
    acc = alpha[:, :1] * acc_ref[...] + p[:, :1] * ln_ref[...].astype(bf16).astype(f32)
    o_ref[...] = (acc / l[:, :1]).astype(o_ref.dtype)

    @pl.when(b == pl.num_programs(0) - 1)
    def _():
        for k in range(ahead):
            for c in page_copies(nsteps - 1, lax.rem(nsteps + k, PAGE_SLOTS)):
                c.wait()


def attn_sample(page_table, qa, qr, qf, knew, latnew, wuk_t, cache_lat, cache_rope_t, *, layer, G):
    B, n_pages = page_table.shape
    ng = n_pages // G
    pt = page_table.reshape(-1)
    per_b = lambda *shape: pl.BlockSpec((None,) + shape, lambda b, pt: (b,) + (0,) * len(shape))
    in_specs = [per_b(MLA_HEADS, MLA_KV_LORA), per_b(MLA_HEADS, MLA_ROPE), per_b(MLA_HEADS, HEAD_PAD),
                per_b(MLA_HEADS, HEAD_PAD), per_b(1, MLA_KV_LORA),
                pl.BlockSpec((MLA_HEADS * MLA_NOPE, MLA_KV_LORA), lambda b, pt: (0, 0)),
                pl.BlockSpec(memory_space=pl.ANY), pl.BlockSpec(memory_space=pl.ANY)]
    return pl.pallas_call(
        functools.partial(_attn_s_kernel, G=G, ng=ng, layer=layer, nsteps=B * ng),
        grid_spec=pltpu.PrefetchScalarGridSpec(
            num_scalar_prefetch=1,
            grid=(B,),
            in_specs=in_specs,
            out_specs=per_b(MLA_HEADS, MLA_KV_LORA),
            scratch_shapes=[pltpu.VMEM((PAGE_SLOTS, G, PAGE_SIZE, MLA_KV_LORA), f32),
                            pltpu.VMEM((PAGE_SLOTS, G, MLA_ROPE, PAGE_SIZE), f32),
                            pltpu.SemaphoreType.DMA((PAGE_SLOTS,)),
                            pltpu.VMEM((MLA_HEADS, 128), f32),
                            pltpu.VMEM((MLA_HEADS, 128), f32),
                            pltpu.VMEM((MLA_HEADS, MLA_KV_LORA), f32)]),
        out_shape=jax.ShapeDtypeStruct((B, MLA_HEADS, MLA_KV_LORA), bf16),
        compiler_params=_cp("arbitrary"),
        name="attn_sample",
    )(pt, qa, qr, qf, knew, latnew, wuk_t, cache_lat, cache_rope_t)


def _rwkv_prep_math(z, zs, mu_ref, wcat_ref, w0_ref, a0_ref, kk_ref, ka_ref, rk_ref, ind_ref, outs):
    r_o, k_o, v_o, kk_o, b_o, lw_o, g_o, bonus_o = outs
    zm = z + (zs - z) * mu_ref[...]
    r = zm[:, :MIX_W]
    k = zm[:, MIX_W:2 * MIX_W]
    v = zm[:, 2 * MIX_W:3 * MIX_W]
    tail = zm[:, 3 * MIX_W:]
    lane = lax.broadcasted_iota(jnp.int32, (1, 256), 1)
    act = jnp.where(lane < 64, jnp.tanh(tail), jnp.where(lane < 128, tail, 1.0 / (1.0 + jnp.exp(-tail))))
    lo = _mm(act.astype(bf16), wcat_ref[...])
    u = -(w0_ref[...] + lo[:, :MIX_W])
    w = -(jnp.maximum(u, 0.0) + jnp.log(1.0 + jnp.exp(-jnp.abs(u)))) - 0.5
    a = 1.0 / (1.0 + jnp.exp(-(a0_ref[...] + lo[:, MIX_W:2 * MIX_W])))
    ind = ind_ref[...]
    kk = k * kk_ref[...]
    kk = kk * lax.rsqrt(_segsum(kk * kk, ind) + 1e-12)
    k_eff = k * (1.0 + (a - 1.0) * ka_ref[...])
    r_o[...] = r
    k_o[...] = k_eff
    v_o[...] = v
    kk_o[...] = kk
    b_o[...] = kk * a
    lw_o[...] = -jnp.exp(w)
    g_o[...] = lo[:, 2 * MIX_W:]
    bonus_o[...] = _segsum(r * k_eff * rk_ref[...], ind) * v


def _rwkv_prep_p_kernel(z_ref, halo_ref, *rest):
    params, outs = rest[:8], rest[8:]
    z = z_ref[...]
    first = pl.program_id(0) == 0
    prev = jnp.where(first, 0.0, halo_ref[7:8, :])
    row = lax.broadcasted_iota(jnp.int32, (z.shape[0], 1), 0)
    zs = jnp.where(row == 0, prev, pltpu.roll(z, 1, 0))
    _rwkv_prep_math(z, zs, *params, outs)


def _rwkv_prep_s_kernel(z_ref, zs_ref, *rest):
    params, outs = rest[:8], rest[8:]
    _rwkv_prep_math(z_ref[...], zs_ref[...], *params, outs)


def rwkv_prep(proj_z, shift, params, *, T, B, tm):
    const = lambda a: pl.BlockSpec(a.shape, lambda i: (0,) * a.ndim)
    pspecs = [const(p) for p in params]
    outs_p = [jax.ShapeDtypeStruct((T, MIX_W), f32)] * 8
    res_p = pl.pallas_call(
        _rwkv_prep_p_kernel,
        grid=(T // tm,),
        in_specs=[pl.BlockSpec((tm, RWKV_PROJ), lambda i: (i, 0)),
                  pl.BlockSpec((8, RWKV_PROJ), lambda i: (jnp.maximum(i * (tm // 8) - 1, 0), 0))] + pspecs,
        out_specs=[pl.BlockSpec((tm, MIX_W), lambda i: (i, 0))] * 8,
        out_shape=outs_p,
        compiler_params=_cp("parallel"),
        name="rwkv_prep_prompt",
    )(proj_z, proj_z, *params)
    res_s = pl.pallas_call(
        _rwkv_prep_s_kernel,
        grid=(1,),
        in_specs=[pl.BlockSpec((B, RWKV_PROJ), lambda i: (T // B, 0)),
                  pl.BlockSpec((B, RWKV_PROJ), lambda i: (0, 0))] + pspecs,
        out_specs=[pl.BlockSpec((B, MIX_W), lambda i: (0, 0))] * 8,
        out_shape=[jax.ShapeDtypeStruct((B, MIX_W), f32)] * 8,
        compiler_params=_cp("arbitrary"),
        name="rwkv_prep_sample",
    )(proj_z, shift, *params)
    return res_p, res_s


def _scan_kernel(r_ref, k_ref, v_ref, kk_ref, b_ref, lw_ref, lwt_ref, bt_ref, kt_ref,
                 y_ref, so_ref, st_ref, *, exact):
    C = SCAN_CHUNK
    N = RWKV_HEAD

    @pl.when(pl.program_id(0) == 0)
    def _():
        st_ref[...] = jnp.zeros_like(st_ref)

    row = lax.broadcasted_iota(jnp.int32, (C, C), 0)
    col = lax.broadcasted_iota(jnp.int32, (C, C), 1)
    strict = col < row
    incl = col <= row
    eye = (row == col).astype(f32)
    if exact:
        op = lambda x: x
        mm = functools.partial(_mm, precision=HI)
        nt = functools.partial(_nt, precision=HI)
    else:
        op = lambda x: x.astype(bf16)
        mm, nt = _mm, _nt

    def chunk(c, carry):
        rows = pl.ds(pl.multiple_of(c * C, C), C)
        lw = lw_ref[rows, :]
        cs = _mm(incl.astype(f32), lw, precision=HI)
        cst = _mm(lwt_ref[c], (row <= col).astype(f32), precision=HI)
        em = jnp.exp(-cs)
        emt = jnp.exp(-cst)
        ar_all = op(jnp.concatenate([-kk_ref[rows, :] * jnp.exp(cs - lw), r_ref[rows, :] * jnp.exp(cs)], axis=0))
        bh_all = op(b_ref[rows, :] * em)
        kh_all = op(k_ref[rows, :] * em)
        bkt_all = op(jnp.concatenate([bt_ref[c] * emt, kt_ref[c] * emt], axis=1))
        g_all = jnp.exp(cst[:, C - 1:C])
        v_all = op(v_ref[rows, :])
        H = range(RWKV_HEADS)
        hsl = [slice(h * N, (h + 1) * N) for h in H]
        ar = [ar_all[:, s] for s in hsl]
        v = [v_all[:, s] for s in hsl]
        pb = [nt(ar[h], bh_all[:, hsl[h]]) for h in H]
        pk = [nt(ar[h], kh_all[:, hsl[h]]) for h in H]
        pw = [jnp.where(strict, pb[h][:C], 0.0) for h in H]
        tinv = [eye + pw[h] for h in H]
        for _ in range(5):
            pw = [mm(op(pw[h]), op(pw[h])) for h in H]
            tinv = [tinv[h] + mm(op(tinv[h]), op(pw[h])) for h in H]
        st = [st_ref[h] for h in H]
        xs = [mm(ar[h], op(st[h])) for h in H]
        pv = [mm(op(jnp.concatenate([jnp.where(strict, pk[h][:C], 0.0), jnp.where(incl, pk[h][C:], 0.0)], axis=0)), v[h])
              for h in H]
        ut = [mm(op(tinv[h]), op(xs[h][:C] + pv[h][:C])) for h in H]
        for h in H:
            y_ref[rows, hsl[h]] = xs[h][C:] + pv[h][C:] + mm(op(jnp.where(incl, pb[h][C:], 0.0)), op(ut[h]))
        for h in H:
            uv = jnp.concatenate([op(ut[h]), v[h]], axis=0)
            st_ref[h] = g_all[hsl[h]] * (st[h] + mm(bkt_all[hsl[h]], uv))
        return carry

    lax.fori_loop(0, lw_ref.shape[0] // C, chunk, 0)
    so_ref[...] = st_ref[...]


def rwkv_scan(r, k, v, kk, b, lw, *, exact):
    T = r.shape[0]
    C = SCAN_CHUNK
    nc = T // C
    per = _tile(nc, SCAN_CHUNKS_PER_STEP, 1)
    tr = lambda x: x.reshape(nc, C, MIX_W).transpose(0, 2, 1)
    row = pl.BlockSpec((per * C, MIX_W), lambda c: (c, 0))
    colm = pl.BlockSpec((per, MIX_W, C), lambda c: (c, 0, 0))
    return pl.pallas_call(
        functools.partial(_scan_kernel, exact=exact),
        grid=(nc // per,),
        in_specs=[row] * 6 + [colm] * 3,
        out_specs=[row, pl.BlockSpec((RWKV_HEADS, RWKV_HEAD, RWKV_HEAD), lambda c: (0, 0, 0))],
        out_shape=[jax.ShapeDtypeStruct((T, MIX_W), f32),
                   jax.ShapeDtypeStruct((RWKV_HEADS, RWKV_HEAD, RWKV_HEAD), f32)],
        scratch_shapes=[pltpu.VMEM((RWKV_HEADS, RWKV_HEAD, RWKV_HEAD), f32)],
        compiler_params=_cp("arbitrary"),
        name="rwkv_scan",
    )(r, k, v, kk, b, lw, tr(lw), tr(b), tr(k))


def _rwkv_step_kernel(s_ref, r_ref, k_ref, v_ref, kk_ref, b_ref, lw_ref, so_ref, y_ref):
    s = s_ref[...]
    sa = jnp.sum(s * (-kk_ref[...])[None], axis=1, keepdims=True)
    s_new = s * jnp.exp(lw_ref[...])[None] + sa * b_ref[...][None] + v_ref[...] * k_ref[...][None]
    so_ref[...] = s_new
    y_ref[...] = jnp.sum(s_new * r_ref[...][None], axis=1, keepdims=True)


def rwkv_step(state, r, k, v, kk, b, lw):
    B = state.shape[0]
    H, N = RWKV_HEADS, RWKV_HEAD
    st = state.transpose(1, 2, 3, 0)
    keyv = lambda x: x.T.reshape(H, N, B)
    kspec = pl.BlockSpec((None, N, B), lambda h: (h, 0, 0))
    cspec = pl.BlockSpec((None, N, 1, B), lambda h: (h, 0, 0, 0))
    sspec = pl.BlockSpec((None, N, N, B), lambda h: (h, 0, 0, 0))
    s_new, y = pl.pallas_call(
        _rwkv_step_kernel,
        grid=(H,),
        in_specs=[sspec, kspec, kspec, cspec, kspec, kspec, kspec],
        out_specs=[sspec, cspec],
        out_shape=[jax.ShapeDtypeStruct((H, N, N, B), f32), jax.ShapeDtypeStruct((H, N, 1, B), f32)],
        compiler_params=_cp("parallel"),
        name="rwkv_step",
    )(st, keyv(r), keyv(k), keyv(v).reshape(H, N, 1, B), keyv(kk), keyv(b), keyv(lw))
    return s_new.transpose(3, 0, 1, 2), y.reshape(H * N, B).T


def _rwkv_post_kernel(y_ref, bonus_ref, g_ref, lnw_ref, lnb_ref, ind_ref, o_ref):
    ind = ind_ref[...]
    y = y_ref[...]
    d = y - _segsum(y, ind) * (1.0 / RWKV_HEAD)
    var = _segsum(d * d, ind) * (1.0 / RWKV_HEAD)
    yn = d * lax.rsqrt(var + GN_EPS) * lnw_ref[...] + lnb_ref[...]
    o_ref[...] = ((yn + bonus_ref[...]) * g_ref[...]).astype(o_ref.dtype)


def rwkv_post(y, bonus, g, ln_w, ln_b, ind, *, tm):
    M = y.shape[0]
    row = pl.BlockSpec((tm, MIX_W), lambda i: (i, 0))
    vec = pl.BlockSpec((1, MIX_W), lambda i: (0, 0))
    return pl.pallas_call(
        _rwkv_post_kernel,
        grid=(M // tm,),
        in_specs=[row, row, row, vec, vec, pl.BlockSpec((MIX_W, MIX_W), lambda i: (0, 0))],
        out_specs=row,
        out_shape=jax.ShapeDtypeStruct((M, MIX_W), bf16),
        compiler_params=_cp("parallel"),
        name="rwkv_post",
    )(y, bonus, g, ln_w, ln_b, ind)


def _merge_kernel(b0_ref, lat_ref, wuv_ref, b2_ref, w_ref, g0_ref, g1_ref, g2_ref, o_ref, b1_ref):
    @pl.when(pl.program_id(1) == 0)
    def _():
        for h in range(MLA_HEADS):
            b1_ref[:, h * MLA_NOPE:(h + 1) * MLA_NOPE] = _mm(
                lat_ref[:, h * MLA_KV_LORA:(h + 1) * MLA_KV_LORA], wuv_ref[h]).astype(bf16)

    acc = None
    for n, (b_ref, g_ref) in enumerate(((b0_ref, g0_ref), (b1_ref, g1_ref), (b2_ref, g2_ref))):
        gate = 1.0 / (1.0 + jnp.exp(-g_ref[...]))
        t = gate * _mm(b_ref[...], w_ref[n])
        acc = t if acc is None else acc + t
    o_ref[...] = acc.astype(o_ref.dtype)


def merge(o_pool, o_lat, w_uv, o_rwkv, w_branch, proj_g, *, tm, tn):
    M = proj_g.shape[0]
    nb = D_MODEL // tn
    bspec = pl.BlockSpec((tm, MIX_W), lambda i, j: (i, 0))
    gspec = lambda n: pl.BlockSpec((tm, tn), lambda i, j: (i, n * nb + j))
    return pl.pallas_call(
        _merge_kernel,
        grid=(M // tm, nb),
        in_specs=[bspec, pl.BlockSpec((tm, MLA_HEADS * MLA_KV_LORA), lambda i, j: (i, 0)),
                  pl.BlockSpec((MLA_HEADS, MLA_KV_LORA, MLA_NOPE), lambda i, j: (0, 0, 0)), bspec,
                  pl.BlockSpec((3, MIX_W, tn), lambda i, j: (0, 0, j)), gspec(0), gspec(1), gspec(2)],
        out_specs=pl.BlockSpec((tm, tn), lambda i, j: (i, j)),
        out_shape=jax.ShapeDtypeStruct((M, D_MODEL), bf16),
        scratch_shapes=[pltpu.VMEM((tm, MIX_W), bf16)],
        compiler_params=_cp("parallel", "arbitrary"),
        name="merge",
    )(o_pool, o_lat, w_uv, o_rwkv, w_branch, proj_g, proj_g, proj_g)


def _memkv_kernel(kv_ref, g_ref, k_ref, v_ref):
    for h in range(MEM_HEADS):
        hs = slice(h * MEM_HEAD_DIM, (h + 1) * MEM_HEAD_DIM)
        k_ref[:, hs] = _rms(kv_ref[:, hs], g_ref[...])
    v_ref[...] = kv_ref[:, MEM_W:]


def memkv_post(kv, g):
    n = kv.shape[0]
    return pl.pallas_call(
        _memkv_kernel,
        grid=(1,),
        in_specs=[pl.BlockSpec((n, 2 * MEM_W), lambda i: (0, 0)), pl.BlockSpec((1, MEM_HEAD_DIM), lambda i: (0, 0))],
        out_specs=[pl.BlockSpec((n, MEM_W), lambda i: (0, 0))] * 2,
        out_shape=[jax.ShapeDtypeStruct((n, MEM_W), f32)] * 2,
        compiler_params=_cp("arbitrary"),
        name="memkv_post",
    )(kv, g)


def _memattn_p_kernel(q_ref, g_ref, k_ref, v_ref, o_ref):
    for h in range(MEM_HEADS):
        hs = slice(h * MEM_HEAD_DIM, (h + 1) * MEM_HEAD_DIM)
        q = _rms(q_ref[:, hs], g_ref[...]).astype(bf16)
        s = _nt(q, k_ref[:, hs].astype(bf16)) * (MEM_HEAD_DIM ** -0.5)
        p = jnp.exp(s - jnp.max(s, axis=1, keepdims=True))
        o = _mm(p.astype(bf16), v_ref[:, hs].astype(bf16)) / jnp.sum(p, axis=1, keepdims=True)
        o_ref[:, hs] = o.astype(o_ref.dtype)


def memattn_prompt(q, g, mem_k, mem_v, *, T, tm):
    n = mem_k.shape[0]
    return pl.pallas_call(
        _memattn_p_kernel,
        grid=(T // tm,),
        in_specs=[pl.BlockSpec((tm, MEM_W), lambda i: (i, 0)), pl.BlockSpec((1, MEM_HEAD_DIM), lambda i: (0, 0)),
                  pl.BlockSpec((n, MEM_W), lambda i: (0, 0)), pl.BlockSpec((n, MEM_W), lambda i: (0, 0))],
        out_specs=pl.BlockSpec((tm, MEM_W), lambda i: (i, 0)),
        out_shape=jax.ShapeDtypeStruct((T, MEM_W), bf16),
        compiler_params=_cp("parallel"),
        name="memattn_prompt",
    )(q, g, mem_k, mem_v)


def _memattn_s_kernel(q_ref, g_ref, k_ref, v_ref, o_ref):
    for h in range(MEM_HEADS):
        hs = slice(h * MEM_HEAD_DIM, (h + 1) * MEM_HEAD_DIM)
        q = _rms(q_ref[:, :, hs], g_ref[...])
        s = jnp.sum(k_ref[:, :, h, :] * q, axis=-1, keepdims=True) * (MEM_HEAD_DIM ** -0.5)
        p = jnp.exp(s - jnp.max(s, axis=1, keepdims=True))
        o = jnp.sum(p * v_ref[:, :, h, :], axis=1, keepdims=True) / jnp.sum(p, axis=1, keepdims=True)
        o_ref[:, :, hs] = o.astype(o_ref.dtype)


def memattn_sample(q, g, cache_k, cache_v, *, layer, T, B, tb):
    n = cache_k.shape[2]
    q3 = q.reshape(q.shape[0], 1, MEM_W)
    ck, cv = cache_k, cache_v
    kvspec = pl.BlockSpec((None, tb, n, MEM_HEADS, MEM_HEAD_DIM), lambda i: (layer, i, 0, 0, 0))
    out = pl.pallas_call(
        _memattn_s_kernel,
        grid=(B // tb,),
        in_specs=[pl.BlockSpec((tb, 1, MEM_W), lambda i: (T // tb + i, 0, 0)),
                  pl.BlockSpec((1, MEM_HEAD_DIM), lambda i: (0, 0)), kvspec, kvspec],
        out_specs=pl.BlockSpec((tb, 1, MEM_W), lambda i: (i, 0, 0)),
        out_shape=jax.ShapeDtypeStruct((B, 1, MEM_W), bf16),
        compiler_params=_cp("parallel"),
        name="memattn_sample",
    )(q3, g, ck, cv)
    return out.reshape(B, MEM_W)


def _ffn1_kernel(x_ref, g_ref, wg_ref, wu_ref, *rest, scaled):
    if scaled:
        c_ref, o_ref, xn_ref = rest
    else:
        o_ref, xn_ref = rest

    @pl.when(pl.program_id(1) == 0)
    def _():
        xn_ref[...] = _rms(x_ref[...], g_ref[...]).astype(bf16)

    xn = xn_ref[...]
    a = _mm(xn, wg_ref[...])
    a = a / (1.0 + jnp.exp(-a)) * _mm(xn, wu_ref[...])
    if scaled:
        a = a * c_ref[:, :1]
    o_ref[...] = a.astype(o_ref.dtype)


def ffn1(x, g, wg, wu, *, tm, tn, comb=None):
    M, K = x.shape
    F = wg.shape[0] * wg.shape[2]
    nper = wg.shape[2] // tn
    wspec = pl.BlockSpec((None, K, tn), lambda i, j: (j // nper, 0, j % nper))
    in_specs = [pl.BlockSpec((tm, K), lambda i, j: (i, 0)), pl.BlockSpec((1, K), lambda i, j: (0, 0)), wspec, wspec]
    args = [x, g, wg, wu]
    if comb is not None:
        in_specs.append(pl.BlockSpec((tm, 128), lambda i, j: (i, j // nper)))
        args.append(comb)
    return pl.pallas_call(
        functools.partial(_ffn1_kernel, scaled=comb is not None),
        grid=(M // tm, F // tn),
        in_specs=in_specs,
        out_specs=pl.BlockSpec((tm, tn), lambda i, j: (i, j)),
        out_shape=jax.ShapeDtypeStruct((M, F), bf16),
        scratch_shapes=[pltpu.VMEM((tm, K), bf16)],
        compiler_params=_cp("parallel", "arbitrary"),
        name="ffn1",
    )(*args)


def _router_kernel(x_ref, g_ref, w_ref, b_ref, e_ref, o_ref):
    logits = _mm(_rms(x_ref[...], g_ref[...]), w_ref[...], precision=HI) + b_ref[...]
    lane = lax.broadcasted_iota(jnp.int32, logits.shape, 1)
    logits = jnp.where(lane < N_EXPERTS, logits, -jnp.inf)
    m1 = jnp.max(logits, axis=1, keepdims=True)
    i1 = jnp.min(jnp.where(logits == m1, lane, 128), axis=1, keepdims=True)
    rest = jnp.where(lane == i1, -jnp.inf, logits)
    m2 = jnp.max(rest, axis=1, keepdims=True)
    i2 = jnp.min(jnp.where(rest == m2, lane, 128), axis=1, keepdims=True)
    e2 = jnp.exp(m2 - m1)
    w1 = 1.0 / (1.0 + e2)
    w2 = e2 / (1.0 + e2)
    comb = jnp.where(lane == i1, w1, 0.0) + jnp.where(lane == i2, w2, 0.0)
    o_ref[...] = _mm(comb, e_ref[...], precision=HI)


def router(x, g, w_pad, b_pad, expand, *, tm):
    M, K = x.shape
    return pl.pallas_call(
        _router_kernel,
        grid=(M // tm,),
        in_specs=[pl.BlockSpec((tm, K), lambda i: (i, 0)), pl.BlockSpec((1, K), lambda i: (0, 0)),
                  pl.BlockSpec((K, 128), lambda i: (0, 0)), pl.BlockSpec((1, 128), lambda i: (0, 0)),
                  pl.BlockSpec((128, N_EXPERTS * 128), lambda i: (0, 0))],
        out_specs=pl.BlockSpec((tm, N_EXPERTS * 128), lambda i: (i, 0)),
        out_shape=jax.ShapeDtypeStruct((M, N_EXPERTS * 128), f32),
        compiler_params=_cp("parallel"),
        name="router",
    )(x, g, w_pad, b_pad, expand)


def _swap_halves(x):
    h = x.shape[-1] // 2
    return jnp.concatenate([x[..., h:], x[..., :h]], axis=-1)


def kernel(x_prompt, x_sample, mem_prompt, state_pool, cache_mla_latent, cache_mla_rope, state_rwkv, state_rwkv_shift, cache_mem_k, cache_mem_v, page_table, norm_mix_g, w_in, pool_w, pool_scale, mla_cq_g, mla_w_uq, mla_ckv_g, mla_kr_g, mla_w_uk, mla_w_uv, mla_qn_g, mla_qr_g, mla_kn_g, rwkv_mu, rwkv_w0, rwkv_w2, rwkv_a0, rwkv_a2, rwkv_g2, rwkv_k_k, rwkv_k_a, rwkv_r_k, rwkv_ln_w, rwkv_ln_b, w_branch, w_out, norm_mem_g, mem_norm_g, w_q_mem, w_k_mem, w_v_mem, mem_qn_g, mem_kn_g, w_o_mem, norm_ffn_g, ffn_w_gate, ffn_w_up, ffn_w_down, moe_router, moe_router_b, moe_w_gate, moe_w_up, moe_w_down):
    depth = w_in.shape[0]
    T = x_prompt.shape[1]
    B = x_sample.shape[0]
    assert x_prompt.shape[0] == 1 and x_sample.shape[1] == 1
    n_pages = page_table.shape[1]
    past_len = n_pages * PAGE_SIZE
    M = T + B
    assert T % B == 0 and T % SCAN_CHUNK == 0 and B % 8 == 0

    tm = _tile(M, 640)
    tp = _tile(T, 512)
    tq = _tile(T, 512, 128)
    tb = _tile(B, 8)
    G = _tile(n_pages, 32, 2)
    row = lambda v: v.reshape(1, -1)

    x = jnp.concatenate([x_prompt[0], x_sample[:, 0]], axis=0)
    cache_rope_t = jnp.swapaxes(cache_mla_rope, 2, 3)

    pos = jnp.concatenate([jnp.arange(T, dtype=jnp.int32), jnp.full((B,), past_len, jnp.int32)])
    inv = ROPE_THETA ** (-jnp.arange(0, MLA_ROPE, 2, dtype=f32) / MLA_ROPE)
    ang = pos.astype(f32)[:, None] * inv[None, :]
    cos, sin = jnp.cos(ang), jnp.sin(ang)
    tab = jnp.concatenate([cos, cos, -sin, sin], axis=1)

    ids = jnp.arange(MIX_W) // RWKV_HEAD
    ind = (ids[:, None] == ids[None, :]).astype(bf16)
    expand = (jnp.arange(128)[:, None] == (jnp.arange(N_EXPERTS * 128) // 128)[None, :]).astype(f32)

    outs = {k: [] for k in ("pool_p", "lat_p", "rope_p", "rwkv_p", "shift_p", "memk_p", "memv_p",
                            "pool_s", "lat_s", "rope_s", "rwkv_s", "shift_s")}
    for l in range(depth):
        wi = w_in[l]
        kr_cols = wi[:, OFF_KR:OFF_RWKV]
        w_a = jnp.concatenate([wi[:, :OFF_KR], kr_cols, _swap_halves(kr_cols)], axis=1).astype(bf16)
        w_z = wi[:, OFF_RWKV:OFF_GATE].astype(bf16)
        w_g = wi[:, OFF_GATE:].astype(bf16)
        uq = mla_w_uq[l].reshape(MLA_Q_LORA, MLA_HEADS, MLA_QK)
        w_big = jnp.concatenate([uq, _swap_halves(uq[..., MLA_NOPE:])], axis=-1)
        w_big = w_big.reshape(MLA_Q_LORA, MLA_HEADS * HEAD_PAD).astype(bf16)
        gq = row(jnp.concatenate([mla_qn_g[l], mla_qr_g[l], _swap_halves(mla_qr_g[l])]) * MLA_SCALE)
        gk = row(jnp.concatenate([mla_kr_g[l], _swap_halves(mla_kr_g[l])]))
        w_uk = mla_w_uk[l].reshape(MLA_KV_LORA, MLA_HEADS * MLA_NOPE)
        w_uk_b = w_uk.astype(bf16)
        wuk_t = w_uk.T.astype(bf16)
        w_abs = (mla_w_uk[l] * mla_kn_g[l][None, None, :]).transpose(1, 2, 0)
        w_abs = jnp.concatenate([w_abs, jnp.zeros((MLA_HEADS, HEAD_PAD - MLA_NOPE, MLA_KV_LORA), f32)], axis=1).astype(bf16)
        w_uv = mla_w_uv[l].transpose(1, 0, 2).astype(bf16)
        wcat = jnp.zeros((256, 3 * MIX_W), f32)
        wcat = wcat.at[:64, :MIX_W].set(rwkv_w2[l]).at[64:128, MIX_W:2 * MIX_W].set(rwkv_a2[l])
        wcat = wcat.at[128:, 2 * MIX_W:].set(rwkv_g2[l]).astype(bf16)
        rparams = [row(rwkv_mu[l]), wcat, row(rwkv_w0[l]), row(rwkv_a0[l]), row(rwkv_k_k[l]), row(rwkv_k_a[l]),
                   row(rwkv_r_k[l]), ind]

        g_mix = row(norm_mix_g[l])
        proj_a = fmm_norm(x, g_mix, w_a, tm=tm, tn=w_a.shape[1])
        proj_z = fmm_norm(x, g_mix, w_z, tm=tm, tn=RWKV_PROJ // 2)
        proj_g = fmm_norm(x, g_mix, w_g, tm=tm, tn=1536)

        pw = pool_w[l].astype(bf16)
        o_pool = jnp.concatenate([
            pool_prompt(proj_a, pw, row(pool_scale[l]), T=T, tm=tp),
            pool_sample(state_pool[l].transpose(1, 0, 2), proj_a, pw, row(pool_scale[l]), T=T, B=B, past_len=past_len)])
        u = proj_a[:, :MIX_W]
        outs["pool_p"].append(u[T - POOL_HIST:T][None])
        outs["pool_s"].append(jnp.concatenate([state_pool[l][:, 1:], u[T:, None]], axis=1))

        q_full = qprep(proj_a, row(mla_cq_g[l]), w_big, gq, tab, tm=tm)
        lat, lat_b, krope, k_full = kprep(proj_a, row(mla_ckv_g[l]), gk, tab, w_uk_b, row(mla_kn_g[l]), tm=tm)
        outs["lat_p"].append(lat[:T][None])
        outs["lat_s"].append(lat[T:, None])
        outs["rope_p"].append(krope[:T, :MLA_ROPE][None])
        outs["rope_s"].append(krope[T:, None, :MLA_ROPE])
        o_lat_p = attn_prompt(q_full, k_full, lat_b, T=T, tq=tq)
        q_s = q_full[T:]
        qa = gmm(q_s, w_abs, tm=B, out_dtype=bf16).reshape(B, MLA_HEADS, MLA_KV_LORA)
        q_s3 = q_s.reshape(B, MLA_HEADS, HEAD_PAD)
        o_lat_s = attn_sample(page_table, qa, q_s3[:, :, MLA_NOPE:MLA_QK], q_s3,
                              k_full[T:].reshape(B, MLA_HEADS, HEAD_PAD), lat[T:, None], wuk_t,
                              cache_mla_latent, cache_rope_t, layer=l, G=G)
        o_lat = jnp.concatenate([o_lat_p, o_lat_s.reshape(B, MLA_HEADS * MLA_KV_LORA)])

        (rp, rs_) = rwkv_prep(proj_z, state_rwkv_shift[l], rparams, T=T, B=B, tm=_tile(T, 256))
        y_p, st_t = rwkv_scan(rp[0], rp[1], rp[2], rp[3], rp[4], rp[5], exact=SCAN_EXACT)
        s_new, y_s = rwkv_step(state_rwkv[l], rs_[0], rs_[1], rs_[2], rs_[3], rs_[4], rs_[5])
        outs["rwkv_p"].append(st_t.transpose(0, 2, 1)[None])
        outs["rwkv_s"].append(s_new)
        outs["shift_p"].append(proj_z[T - 1:T])
        outs["shift_s"].append(proj_z[T:])
        cat = lambda a, b: jnp.concatenate([a, b])
        o_rwkv = rwkv_post(cat(y_p, y_s), cat(rp[7], rs_[7]), cat(rp[6], rs_[6]),
                           row(rwkv_ln_w[l]), row(rwkv_ln_b[l]), ind, tm=tm)

        merged = merge(o_pool, o_lat, w_uv, o_rwkv, w_branch[l].astype(bf16), proj_g, tm=tm, tn=512)
        x = fmm(merged, w_out[l].astype(bf16), tm=tm, tn=1024, tk=D_MODEL, res=x)

        w_kv = jnp.concatenate([w_k_mem[l], w_v_mem[l]], axis=1).astype(bf16)
        n_mem = mem_prompt.shape[1]
        kv = fmm_norm(mem_prompt[0], row(mem_norm_g[l]), w_kv, tm=n_mem, tn=2 * MEM_W)
        mem_k, mem_v = memkv_post(kv, row(mem_kn_g[l]))
        outs["memk_p"].append(mem_k.reshape(1, n_mem, MEM_HEADS, MEM_HEAD_DIM))
        outs["memv_p"].append(mem_v.reshape(1, n_mem, MEM_HEADS, MEM_HEAD_DIM))
        q_mem = fmm_norm(x, row(norm_mem_g[l]), w_q_mem[l].astype(bf16), tm=tm, tn=MEM_W)
        o_mem = jnp.concatenate([
            memattn_prompt(q_mem, row(mem_qn_g[l]), mem_k, mem_v, T=T, tm=tp),
            memattn_sample(q_mem, row(mem_qn_g[l]), cache_mem_k, cache_mem_v, layer=l, T=T, B=B, tb=tb)])
        x = fmm(o_mem, w_o_mem[l].astype(bf16), tm=tm, tn=1024, tk=MEM_W, res=x)

        g_ffn = row(norm_ffn_g[l])
        if l % 2 == 0:
            d = l // 2
            a = ffn1(x, g_ffn, ffn_w_gate[d].astype(bf16)[None], ffn_w_up[d].astype(bf16)[None], tm=tm, tn=1408)
            x = fmm(a, ffn_w_down[d].astype(bf16), tm=tm, tn=1024, tk=1408, res=x)
        else:
            e = l // 2
            w_r = jnp.pad(moe_router[e], ((0, 0), (0, 128 - N_EXPERTS)))
            b_r = row(jnp.pad(moe_router_b[e], (0, 128 - N_EXPERTS)))
            comb = router(x, g_ffn, w_r, b_r, expand, tm=tm)
            fe = moe_w_gate.shape[-1]
            wd = moe_w_down[e].reshape(N_EXPERTS * fe, D_MODEL).astype(bf16)
            a = ffn1(x, g_ffn, moe_w_gate[e].astype(bf16), moe_w_up[e].astype(bf16), tm=tm, tn=1408, comb=comb)
            x = fmm(a, wd, tm=tm, tn=1024, tk=fe, res=x)

    st = lambda k: jnp.stack(outs[k])
    return (x[:T][None], x[T:, None],
            st("pool_p"), st("lat_p"), st("rope_p"), st("rwkv_p"), st("shift_p"), st("memk_p"), st("memv_p"),
            st("pool_s"), st("lat_s"), st("rope_s"), st("rwkv_s"), st("shift_s"))
```
